```python
import math
import jax, jax.numpy as jnp
from jax import lax
import numpy as np

D_MODEL = 1024
BATCH = 4
SEQ = 4096
DEPTH = 1
DEC_BATCH = 8
DEC_SEQ = 8192
PAST_LEN = 128

A_HEADS = 8
A_HEAD_DIM = 64
A_VALUE_DIM = 2 * A_HEAD_DIM
A_QK = A_HEADS * 2 * A_HEAD_DIM
A_WIDTH = A_HEADS * A_VALUE_DIM
Q_BLOCK = 128
NUM_BUCKETS = 32
MAX_DISTANCE = 128
G_HEADS = 8
G_KEY_DIM = 128
G_VALUE_DIM = 128
G_KWIDTH = G_HEADS * G_KEY_DIM
G_VWIDTH = G_HEADS * G_VALUE_DIM
CHUNK = 64
D_FF = 4 * D_MODEL
EPS = 1e-6

IN_SPLITS = (A_QK, A_QK, A_WIDTH, G_KWIDTH, G_KWIDTH, G_KWIDTH, G_VWIDTH, G_VWIDTH, D_MODEL, D_MODEL)
IN_WIDTH = sum(IN_SPLITS)
SPLIT_IDX = tuple(int(s) for s in np.cumsum(IN_SPLITS)[:-1])

kernel_name = 'hybrid_diffattn_hgrn2_gated_encoder'


def rmsnorm(x, g):
    xf = x.astype(jnp.float32)
    y = xf * lax.rsqrt(jnp.mean(xf * xf, axis=-1, keepdims=True) + EPS)
    return (y * g.astype(jnp.float32)).astype(x.dtype)


def t5_bucket(rel):
    nb = NUM_BUCKETS // 2
    ret = (rel > 0).astype(jnp.int32) * nb
    n = jnp.abs(rel)
    max_exact = nb // 2
    is_small = n < max_exact
    nf = jnp.maximum(n, 1).astype(jnp.float32)
    large = max_exact + (jnp.log(nf / max_exact) / math.log(MAX_DISTANCE / max_exact)
                         * (nb - max_exact)).astype(jnp.int32)
    large = jnp.minimum(large, nb - 1)
    return ret + jnp.where(is_small, n, large)


def diff_attention(q, k, v, rel_bias, lam, lam_init, g_sub):
    B, T = q.shape[0], q.shape[1]
    nblk = T // Q_BLOCK
    scale = A_HEAD_DIM ** -0.5
    kpos = jnp.arange(T, dtype=jnp.int32)
    qb = q.reshape(B, nblk, Q_BLOCK, A_HEADS, 2, A_HEAD_DIM).transpose(1, 0, 2, 3, 4, 5)
    starts = jnp.arange(nblk, dtype=jnp.int32) * Q_BLOCK

    def block(args):
        qblk, start = args
        s = jnp.einsum('bqhcd,bkhcd->bhcqk', qblk, k).astype(jnp.float32) * scale
        qpos = start + jnp.arange(Q_BLOCK, dtype=jnp.int32)
        bucket = t5_bucket(kpos[None, :] - qpos[:, None])
        bias = jnp.transpose(rel_bias[bucket].astype(jnp.float32), (2, 0, 1))
        p = jax.nn.softmax(s + bias[None, :, None], axis=-1)
        w = p[:, :, 0] - lam * p[:, :, 1]
        return jnp.einsum('bhqk,bkhv->bqhv', w.astype(v.dtype), v)

    o = lax.map(block, (qb, starts))
    o = o.transpose(1, 0, 2, 3, 4).reshape(B, T, A_HEADS, A_VALUE_DIM)
    o = rmsnorm(o, g_sub) * (1.0 - lam_init)
    return o.reshape(B, T, A_WIDTH)


def gla_chunk_scan(q, k, v, g):
    B, H, T, dk = q.shape
    dv = v.shape[-1]
    n = T // CHUNK

    def chunks(a):
        return jnp.moveaxis(a.reshape(B, H, n, CHUNK, a.shape[-1]), 2, 0)

    mask = jnp.tril(jnp.ones((CHUNK, CHUNK), dtype=bool))[..., None]

    def step(S, inp):
        qc, kc, vc, gc = inp
        b = jnp.cumsum(gc, axis=-2)
        diff = b[..., :, None, :] - b[..., None, :, :]
        decay = jnp.where(mask, jnp.exp(jnp.where(mask, diff, 0.0)), 0.0)
        A = jnp.einsum('bhtk,bhsk,bhtsk->bhts', qc, kc, decay)
        o = (jnp.einsum('bhtk,bhkv->bhtv', qc * jnp.exp(b), S)
             + jnp.einsum('bhts,bhsv->bhtv', A, vc))
        bl = b[..., -1:, :]
        S = (jnp.exp(bl)[..., 0, :, None] * S
             + jnp.einsum('bhsk,bhsv->bhkv', kc * jnp.exp(bl - b), vc))
        return S, o

    S0 = jnp.zeros((B, H, dk, dv), jnp.float32)
    _, o = lax.scan(step, S0, (chunks(q), chunks(k), chunks(v), chunks(g)))
    return jnp.moveaxis(o, 0, 2).reshape(B, H, T, dv)


def hgrn2_bidir(q, zf_fwd, zf_bwd, i, og, lb_f, lb_b, g_out):
    B, T = q.shape[0], q.shape[1]

    def heads(a, d):
        return a.astype(jnp.float32).reshape(B, T, G_HEADS, d).transpose(0, 2, 1, 3)

    qh = heads(q, G_KEY_DIM)
    vh = heads(i, G_VALUE_DIM)

    def direction(zf, lb, flip):
        f = lb + (1.0 - lb) * jax.nn.sigmoid(zf.astype(jnp.float32))
        kh = heads(1.0 - f, G_KEY_DIM)
        gh = heads(jnp.log(f), G_KEY_DIM)
        if flip:
            o = gla_chunk_scan(jnp.flip(qh, 2), jnp.flip(kh, 2), jnp.flip(vh, 2), jnp.flip(gh, 2))
            return jnp.flip(o, 2)
        return gla_chunk_scan(qh, kh, vh, gh)

    o = direction(zf_fwd, lb_f, False) + direction(zf_bwd, lb_b, True)
    o = o.transpose(0, 2, 1, 3)
    gate = jax.nn.silu(og.astype(jnp.float32)).reshape(B, T, G_HEADS, G_VALUE_DIM)
    o = rmsnorm(o, g_out) * gate
    return o.reshape(B, T, G_VWIDTH).astype(q.dtype)


def encoder_layer(x, l, rel_bias, g_mix_pre, w_in, lam_q1, lam_k1, lam_q2, lam_k2, g_attn_sub,
                  lb_fwd, lb_bwd, g_hgrn_out, w_proj_a, w_proj_b, w_out, g_mix_post,
                  g_mlp_pre, w_mlp_up, w_mlp_down, g_mlp_post):
    B, T = x.shape[0], x.shape[1]
    h = rmsnorm(x, g_mix_pre[l])
    proj = h @ w_in[l]
    aq, ak, av, gq, gff, gfb, gi, gog, ga, gb = jnp.split(proj, SPLIT_IDX, axis=-1)

    lam_init = 0.8 - 0.6 * math.exp(-0.3 * l)
    lam = (jnp.exp(jnp.sum(lam_q1[l].astype(jnp.float32) * lam_k1[l].astype(jnp.float32)))
           - jnp.exp(jnp.sum(lam_q2[l].astype(jnp.float32) * lam_k2[l].astype(jnp.float32)))
           + lam_init)
    o_a = diff_attention(aq.reshape(B, T, A_HEADS, 2, A_HEAD_DIM),
                         ak.reshape(B, T, A_HEADS, 2, A_HEAD_DIM),
                         av.reshape(B, T, A_HEADS, A_VALUE_DIM),
                         rel_bias, lam, lam_init, g_attn_sub[l])

    lb_f = jnp.cumsum(jax.nn.softmax(lb_fwd.astype(jnp.float32), axis=0), axis=0)[l]
    lb_b = jnp.cumsum(jax.nn.softmax(lb_bwd.astype(jnp.float32), axis=0), axis=0)[l]
    o_b = hgrn2_bidir(gq, gff, gfb, gi, gog, lb_f, lb_b, g_hgrn_out[l])

    merged = jax.nn.sigmoid(ga) * (o_a @ w_proj_a[l]) + jax.nn.sigmoid(gb) * (o_b @ w_proj_b[l])
    x = x + rmsnorm(merged @ w_out[l], g_mix_post[l])

    h = rmsnorm(x, g_mlp_pre[l])
    u = jnp.square(jax.nn.relu(h @ w_mlp_up[l]))
    return x + rmsnorm(u @ w_mlp_down[l], g_mlp_post[l])


def setup_inputs(seed: int = 0) -> dict:
    key = jax.random.key(seed)
    ks = jax.random.split(key, 24)
    nrm = jax.random.normal

    def gain(k, d):
        return 1.0 + 0.05 * nrm(k, (DEPTH, d), jnp.float32)

    return {
        'x_prompt': nrm(ks[0], (BATCH, SEQ, D_MODEL), jnp.float32),
        'x_sample': nrm(ks[1], (DEC_BATCH, DEC_SEQ, D_MODEL), jnp.float32),
        'rel_bias': 0.5 * nrm(ks[2], (NUM_BUCKETS, A_HEADS), jnp.float32),
        'g_mix_pre': gain(ks[3], D_MODEL),
        'w_in': nrm(ks[4], (DEPTH, D_MODEL, IN_WIDTH), jnp.float32) * D_MODEL ** -0.5,
        'lam_q1': 0.1 * nrm(ks[5], (DEPTH, A_HEAD_DIM), jnp.float32),
        'lam_k1': 0.1 * nrm(ks[6], (DEPTH, A_HEAD_DIM), jnp.float32),
        'lam_q2': 0.1 * nrm(ks[7], (DEPTH, A_HEAD_DIM), jnp.float32),
        'lam_k2': 0.1 * nrm(ks[8], (DEPTH, A_HEAD_DIM), jnp.float32),
        'g_attn_sub': gain(ks[9], A_VALUE_DIM),
        'lb_fwd': 0.5 * nrm(ks[10], (DEPTH + 1, G_KWIDTH), jnp.float32),
        'lb_bwd': 0.5 * nrm(ks[11], (DEPTH + 1, G_KWIDTH), jnp.float32),
        'g_hgrn_out': gain(ks[12], G_VALUE_DIM),
        'w_proj_a': nrm(ks[13], (DEPTH, A_WIDTH, D_MODEL), jnp.float32) * A_WIDTH ** -0.5,
        'w_proj_b': nrm(ks[14], (DEPTH, G_VWIDTH, D_MODEL), jnp.float32) * G_VWIDTH ** -0.5,
        'w_out': nrm(ks[15], (DEPTH, D_MODEL, D_MODEL), jnp.float32) * D_MODEL ** -0.5,
        'g_mix_post': gain(ks[16], D_MODEL),
        'g_mlp_pre': gain(ks[17], D_MODEL),
        'w_mlp_up': nrm(ks[18], (DEPTH, D_MODEL, D_FF), jnp.float32) * D_MODEL ** -0.5,
        'w_mlp_down': nrm(ks[19], (DEPTH, D_FF, D_MODEL), jnp.float32) * D_FF ** -0.5,
        'g_mlp_post': gain(ks[20], D_MODEL),
    }


def reference(x_prompt, x_sample, rel_bias, g_mix_pre, w_in, lam_q1, lam_k1, lam_q2, lam_k2,
              g_attn_sub, lb_fwd, lb_bwd, g_hgrn_out, w_proj_a, w_proj_b, w_out, g_mix_post,
              g_mlp_pre, w_mlp_up, w_mlp_down, g_mlp_post):
    def trunk(x):
        for l in range(DEPTH):
            x = encoder_layer(x, l, rel_bias, g_mix_pre, w_in, lam_q1, lam_k1, lam_q2, lam_k2,
                              g_attn_sub, lb_fwd, lb_bwd, g_hgrn_out, w_proj_a, w_proj_b, w_out,
                              g_mix_post, g_mlp_pre, w_mlp_up, w_mlp_down, g_mlp_post)
        return x

    y_prompt = trunk(x_prompt)
    y_sample = trunk(x_sample)
    return (y_prompt, y_sample)
```

```python
import functools
import math

import jax
import jax.numpy as jnp
from jax import lax
from jax.experimental import pallas as pl
from jax.experimental.pallas import tpu as pltpu

D_MODEL = 1024
HEADS = 8
HEAD_W = 128
QK_DIM = 64
NUM_BUCKETS = 32
D_FF = 4 * D_MODEL
EPS = 1e-6
CHUNK = 64
LANES = 128
VMEM_LIMIT = 56 * 1024 * 1024

F32 = jnp.float32
BF16 = jnp.bfloat16


def _cparams(sem):
    return pltpu.CompilerParams(dimension_semantics=sem, vmem_limit_bytes=VMEM_LIMIT)


def _rms(x, g):
    return x * lax.rsqrt(jnp.mean(x * x, axis=-1, keepdims=True) + EPS) * g


def _dot_nt(a, b):
    return lax.dot_general(a, b, (((1,), (1,)), ((), ())), preferred_element_type=F32)


def _dot_tn(a, b):
    return lax.dot_general(a, b, (((0,), (0,)), ((), ())), preferred_element_type=F32)


def _inproj_kernel(x_ref, g_ref, w_ref, o_ref, xn_ref):
    @pl.when(pl.program_id(1) == 0)
    def _():
        xn_ref[...] = _rms(x_ref[...], g_ref[...]).astype(BF16)

    o_ref[...] = jnp.dot(xn_ref[...], w_ref[...], preferred_element_type=F32).astype(o_ref.dtype)


def _inproj(x2, g, w, out_dtype):
    n, d = x2.shape
    nout = w.shape[1]
    tm = min(1024, n)
    tn = 1024
    return pl.pallas_call(
        _inproj_kernel,
        grid=(n // tm, nout // tn),
        in_specs=[
            pl.BlockSpec((tm, d), lambda i, j: (i, 0)),
            pl.BlockSpec((1, d), lambda i, j: (0, 0)),
            pl.BlockSpec((d, tn), lambda i, j: (0, j)),
        ],
        out_specs=pl.BlockSpec((tm, tn), lambda i, j: (i, j)),
        out_shape=jax.ShapeDtypeStruct((n, nout), out_dtype),
        scratch_shapes=[pltpu.VMEM((tm, d), BF16)],
        compiler_params=_cparams(("parallel", "arbitrary")),
        name="inproj",
    )(x2, g, w)


def _t5_bias_row(d, relb_ref, h):
    n = jnp.abs(d)
    n2 = n * n
    big = jnp.full(d.shape, 8, jnp.int32)
    for m in range(1, 8):
        big = big + (n2 >= 64 * 2 ** m).astype(jnp.int32)
    big = jnp.minimum(big, NUM_BUCKETS // 2 - 1)
    bucket = jnp.where(d > 0, NUM_BUCKETS // 2, 0) + jnp.where(n < 8, n, big)
    val = jnp.zeros(d.shape, F32)
    for b in range(NUM_BUCKETS):
        val = jnp.where(bucket == b, relb_ref[b, h], val)
    return val


def _attn_kernel(relb_ref, q_ref, k_ref, v_ref, gsub_ref, lq1_ref, lk1_ref, lq2_ref, lk2_ref,
                 o_ref, q2_ref, s_ref, m_ref, l_ref, acc_ref, *, tq, tk, lam_init):
    h = pl.program_id(1)
    qi = pl.program_id(2)
    kj = pl.program_id(3)
    nk = pl.num_programs(3)

    @pl.when(kj == 0)
    def _():
        q = q_ref[0] * (QK_DIM ** -0.5)
        lane = lax.broadcasted_iota(jnp.int32, q.shape, 1)
        zero = jnp.zeros_like(q)
        q2_ref[0:tq, :] = jnp.where(lane < QK_DIM, q, zero)
        q2_ref[tq:2 * tq, :] = jnp.where(lane >= QK_DIM, q, zero)
        m_ref[...] = jnp.full(m_ref.shape, -jnp.inf, F32)
        l_ref[...] = jnp.zeros(l_ref.shape, F32)
        acc_ref[...] = jnp.zeros(acc_ref.shape, F32)

    s_ref[...] = _dot_nt(q2_ref[...], k_ref[0])

    delta = kj * tk - qi * tq
    near = jnp.logical_and(delta > -(tk + 128), delta < tq + 128)
    far_bias = jnp.where(delta < 0, relb_ref[NUM_BUCKETS // 2 - 1, h], relb_ref[NUM_BUCKETS - 1, h])
    c = jnp.where(near, 0.0, far_bias)

    @pl.when(near)
    def _():
        width = tq + tk
        col = lax.broadcasted_iota(jnp.int32, (1, width), 1)
        row_bias = _t5_bias_row(delta + col - tq, relb_ref, h)
        tab = jnp.broadcast_to(row_bias, (tq, width))
        tab = pltpu.roll(tab, 0, 1, stride=1, stride_axis=0)
        bias = tab[:, tq:tq + tk]
        s_ref[0:tq, :] += bias
        s_ref[tq:2 * tq, :] += bias

    s = s_ref[...]
    m_prev = m_ref[...]
    m_new = jnp.maximum(m_prev, jnp.max(s, axis=1, keepdims=True) + c)
    alpha = jnp.exp(m_prev - m_new)
    p = jnp.exp(s - (m_new - c))
    l_ref[...] = alpha * l_ref[...] + jnp.sum(p, axis=1, keepdims=True)
    acc_ref[...] = alpha * acc_ref[...] + jnp.dot(p.astype(BF16), v_ref[0],
                                                  preferred_element_type=F32)
    m_ref[...] = m_new

    @pl.when(kj == nk - 1)
    def _():
        lam = (jnp.exp(jnp.sum(lq1_ref[...] * lk1_ref[...], axis=1, keepdims=True))
               - jnp.exp(jnp.sum(lq2_ref[...] * lk2_ref[...], axis=1, keepdims=True))
               + lam_init)
        o_all = acc_ref[...] / l_ref[...]
        o = o_all[0:tq, :] - lam * o_all[tq:2 * tq, :]
        o_ref[0] = (_rms(o, gsub_ref[...]) * (1.0 - lam_init)).astype(o_ref.dtype)


def _attention(qkv, rel_bias, g_sub, lq1, lk1, lq2, lk2, lam_init):
    b, t, _ = qkv.shape
    tq = min(256, t)
    tk = min(512, t)
    small = pl.BlockSpec((1, QK_DIM), lambda bi, h, qi, kj: (0, 0))
    kern = functools.partial(_attn_kernel, tq=tq, tk=tk, lam_init=lam_init)
    return pl.pallas_call(
        kern,
        grid=(b, HEADS, t // tq, t // tk),
        in_specs=[
            pl.BlockSpec(memory_space=pltpu.SMEM),
            pl.BlockSpec((1, tq, HEAD_W), lambda bi, h, qi, kj: (bi, qi, h)),
            pl.BlockSpec((1, tk, HEAD_W), lambda bi, h, qi, kj: (bi, kj, HEADS + h)),
            pl.BlockSpec((1, tk, HEAD_W), lambda bi, h, qi, kj: (bi, kj, 2 * HEADS + h)),
            pl.BlockSpec((1, HEAD_W), lambda bi, h, qi, kj: (0, 0)),
            small, small, small, small,
        ],
        out_specs=pl.BlockSpec((1, tq, HEAD_W), lambda bi, h, qi, kj: (bi, qi, h)),
        out_shape=jax.ShapeDtypeStruct((b, t, HEADS * HEAD_W), BF16),
        scratch_shapes=[
            pltpu.VMEM((2 * tq, HEAD_W), BF16),
            pltpu.VMEM((2 * tq, tk), F32),
            pltpu.VMEM((2 * tq, 1), F32),
            pltpu.VMEM((2 * tq, 1), F32),
            pltpu.VMEM((2 * tq, HEAD_W), F32),
        ],
        compiler_params=_cparams(("parallel", "parallel", "parallel", "arbitrary")),
        name="diff_attention",
    )(rel_bias, qkv, qkv, qkv, g_sub, lq1, lk1, lq2, lk2)


def _shift_rows(x, k):
    return pltpu.roll(x, k % x.shape[0], 0)


def _block_bcast(x, pos, c, rev):
    src = c if rev else c - 1
    v = jnp.where(pos == src, x, 0.0)
    k = 1
    while k < c:
        v = v + _shift_rows(v, k if rev else -k)
        k *= 2
    return v + _shift_rows(v, -c if rev else c)


def _hgrn_chunk(q, kk, g, v, st, rev):
    shape = q.shape
    row = lax.broadcasted_iota(jnp.int32, shape, 0)
    ti = lax.broadcasted_iota(jnp.int32, (CHUNK, CHUNK), 0)
    si = lax.broadcasted_iota(jnp.int32, (CHUNK, CHUNK), 1)
    qb = q.astype(BF16)
    kb = kk.astype(BF16)
    a = jnp.where(ti == si, _dot_nt(qb, kb), 0.0)
    state = g
    c = 1
    while c < CHUNK:
        pos = row & (2 * c - 1)
        is_q = (pos < c) if rev else (pos >= c)
        tp = _block_bcast(state, pos, c, rev)
        e = jnp.exp(jnp.where(is_q, state, tp - state))
        qf = jnp.where(is_q, q * e, 0.0).astype(BF16)
        kf = jnp.where(is_q, 0.0, kk * e).astype(BF16)
        shift = int(math.log2(2 * c))
        same = (ti >> shift) == (si >> shift)
        a = a + jnp.where(same, _dot_nt(qf, kf), 0.0)
        state = state + jnp.where(is_q, tp, 0.0)
        c *= 2
    edge = 0 if rev else CHUNK - 1
    tot = state[edge:edge + 1, :]
    qe = (q * jnp.exp(state)).astype(BF16)
    kd = (kk * jnp.exp(tot - state)).astype(BF16)
    vb = v.astype(BF16)
    o = _dot_nt(qe, st.astype(BF16)) + jnp.dot(a.astype(BF16), vb, preferred_element_type=F32)
    st_new = st * jnp.exp(tot) + _dot_tn(vb, kd)
    return o, st_new


def _hgrn_scan(q_ref, z_ref, i_ref, lb_ref, st_ref, emit, *, tb, rev):
    @pl.when(pl.program_id(2) == 0)
    def _():
        st_ref[...] = jnp.zeros(st_ref.shape, F32)

    lbx = lb_ref[...]
    ex = jnp.exp(lbx - jnp.max(lbx, axis=0, keepdims=True))
    lb = ex[0:1, :] / jnp.sum(ex, axis=0, keepdims=True)
    nchunk = tb // CHUNK

    def body(j, carry):
        jj = (nchunk - 1 - j) if rev else j
        r0 = pl.multiple_of(jj * CHUNK, CHUNK)
        f = lb + (1.0 - lb) * jax.nn.sigmoid(z_ref[0, pl.ds(r0, CHUNK), :])
        o, st_new = _hgrn_chunk(q_ref[0, pl.ds(r0, CHUNK), :], 1.0 - f, jnp.log(f),
                                i_ref[0, pl.ds(r0, CHUNK), :], st_ref[...], rev)
        st_ref[...] = st_new
        emit(r0, o)
        return carry

    lax.fori_loop(0, nchunk, body, 0)


def _hgrn_fwd_kernel(q_ref, z_ref, i_ref, lb_ref, o_ref, st_ref, *, tb):
    def emit(r0, o):
        o_ref[0, pl.ds(r0, CHUNK), :] = o

    _hgrn_scan(q_ref, z_ref, i_ref, lb_ref, st_ref, emit, tb=tb, rev=False)


def _hgrn_bwd_kernel(q_ref, z_ref, i_ref, lb_ref, of_ref, og_ref, gout_ref, o_ref, st_ref, *, tb):
    def emit(r0, o):
        rows = pl.ds(r0, CHUNK)
        y = _rms(o + of_ref[0, rows, :], gout_ref[...])
        o_ref[0, rows, :] = (y * jax.nn.silu(og_ref[0, rows, :])).astype(o_ref.dtype)

    _hgrn_scan(q_ref, z_ref, i_ref, lb_ref, st_ref, emit, tb=tb, rev=True)


def _hgrn2(proj, lb_fwd, lb_bwd, g_out):
    b, t, _ = proj.shape
    tb = min(1024, t)
    nb = t // tb

    def col(blk, rev):
        if rev:
            return pl.BlockSpec((1, tb, HEAD_W), lambda bi, h, ti: (bi, nb - 1 - ti, blk * HEADS + h))
        return pl.BlockSpec((1, tb, HEAD_W), lambda bi, h, ti: (bi, ti, blk * HEADS + h))

    lb_spec = pl.BlockSpec((lb_fwd.shape[0], HEAD_W), lambda bi, h, ti: (0, h))
    sem = _cparams(("parallel", "parallel", "arbitrary"))
    o_fwd = pl.pallas_call(
        functools.partial(_hgrn_fwd_kernel, tb=tb),
        grid=(b, HEADS, nb),
        in_specs=[col(0, False), col(1, False), col(3, False), lb_spec],
        out_specs=pl.BlockSpec((1, tb, HEAD_W), lambda bi, h, ti: (bi, ti, h)),
        out_shape=jax.ShapeDtypeStruct((b, t, HEADS * HEAD_W), F32),
        scratch_shapes=[pltpu.VMEM((HEAD_W, HEAD_W), F32)],
        compiler_params=sem,
        name="hgrn_fwd",
    )(proj, proj, proj, lb_fwd)
    rev_out = pl.BlockSpec((1, tb, HEAD_W), lambda bi, h, ti: (bi, nb - 1 - ti, h))
    return pl.pallas_call(
        functools.partial(_hgrn_bwd_kernel, tb=tb),
        grid=(b, HEADS, nb),
        in_specs=[col(0, True), col(2, True), col(3, True), lb_spec, rev_out, col(4, True),
                  pl.BlockSpec((1, HEAD_W), lambda bi, h, ti: (0, 0))],
        out_specs=rev_out,
        out_shape=jax.ShapeDtypeStruct((b, t, HEADS * HEAD_W), BF16),
        scratch_shapes=[pltpu.VMEM((HEAD_W, HEAD_W), F32)],
        compiler_params=sem,
        name="hgrn_bwd",
    )(proj, proj, proj, lb_bwd, o_fwd, proj, g_out)


def _merge_kernel(x_ref, oa_ref, ob_ref, ga_ref, gb_ref, wa_ref, wb_ref, wo_ref, g_ref, o_ref):
    pa = jnp.dot(oa_ref[...], wa_ref[...], preferred_element_type=F32)
    pb = jnp.dot(ob_ref[...], wb_ref[...], preferred_element_type=F32)
    merged = jax.nn.sigmoid(ga_ref[...]) * pa + jax.nn.sigmoid(gb_ref[...]) * pb
    t = jnp.dot(merged.astype(BF16), wo_ref[...], preferred_element_type=F32)
    o_ref[...] = x_ref[...] + _rms(t, g_ref[...])


def _merge(x2, oa, ob, proj2, wa, wb, wo, g):
    n, d = x2.shape
    tm = min(512, n)
    row = lambda blk: pl.BlockSpec((tm, d), lambda i: (i, blk))
    full = pl.BlockSpec((d, d), lambda i: (0, 0))
    return pl.pallas_call(
        _merge_kernel,
        grid=(n // tm,),
        in_specs=[row(0), row(0), row(0), row(5), row(6), full, full, full,
                  pl.BlockSpec((1, d), lambda i: (0, 0))],
        out_specs=row(0),
        out_shape=jax.ShapeDtypeStruct((n, d), F32),
        compiler_params=_cparams(("parallel",)),
        name="merge_out",
    )(x2, oa, ob, proj2, proj2, wa, wb, wo, g)


def _mlp_kernel(x_ref, gpre_ref, wu_ref, wd_ref, gpost_ref, o_ref, hn_ref, acc_ref):
    j = pl.program_id(1)

    @pl.when(j == 0)
    def _():
        hn_ref[...] = _rms(x_ref[...], gpre_ref[...]).astype(BF16)
        acc_ref[...] = jnp.zeros(acc_ref.shape, F32)

    u = jnp.square(jnp.maximum(jnp.dot(hn_ref[...], wu_ref[...], preferred_element_type=F32), 0.0))
    acc_ref[...] += jnp.dot(u.astype(BF16), wd_ref[...], preferred_element_type=F32)

    @pl.when(j == pl.num_programs(1) - 1)
    def _():
        o_ref[...] = x_ref[...] + _rms(acc_ref[...], gpost_ref[...])


def _mlp(x2, gpre, wu, wd, gpost):
    n, d = x2.shape
    tm = min(1024, n)
    tf = 1024
    vec = pl.BlockSpec((1, d), lambda i, j: (0, 0))
    return pl.pallas_call(
        _mlp_kernel,
        grid=(n // tm, D_FF // tf),
        in_specs=[pl.BlockSpec((tm, d), lambda i, j: (i, 0)), vec,
                  pl.BlockSpec((d, tf), lambda i, j: (0, j)),
                  pl.BlockSpec((tf, d), lambda i, j: (j, 0)), vec],
        out_specs=pl.BlockSpec((tm, d), lambda i, j: (i, 0)),
        out_shape=jax.ShapeDtypeStruct((n, d), F32),
        scratch_shapes=[pltpu.VMEM((tm, d), BF16), pltpu.VMEM((tm, d), F32)],
        compiler_params=_cparams(("parallel", "arbitrary")),
        name="mlp",
    )(x2, gpre, wu, wd, gpost)


def _encoder_layer(x, l, p):
    b, t, d = x.shape
    n = b * t
    x2 = x.reshape(n, d)
    n_attn = 3 * HEADS * HEAD_W
    w_in = p["w_in"][l]
    qkv = _inproj(x2, p["g_mix_pre"][l:l + 1], w_in[:, :n_attn].astype(BF16), BF16)
    proj = _inproj(x2, p["g_mix_pre"][l:l + 1], w_in[:, n_attn:].astype(BF16), F32)
    lam_init = 0.8 - 0.6 * math.exp(-0.3 * l)
    o_a = _attention(qkv.reshape(b, t, n_attn), p["rel_bias"], p["g_attn_sub"][l:l + 1],
                     p["lam_q1"][l:l + 1], p["lam_k1"][l:l + 1], p["lam_q2"][l:l + 1],
                     p["lam_k2"][l:l + 1], lam_init)
    o_b = _hgrn2(proj.reshape(b, t, proj.shape[1]), p["lb_fwd"], p["lb_bwd"], p["g_hgrn_out"][l:l + 1])
    x1 = _merge(x2, o_a.reshape(n, d), o_b.reshape(n, d), proj,
                p["w_proj_a"][l].astype(BF16), p["w_proj_b"][l].astype(BF16),
                p["w_out"][l].astype(BF16), p["g_mix_post"][l:l + 1])
    y = _mlp(x1, p["g_mlp_pre"][l:l + 1], p["w_mlp_up"][l].astype(BF16),
             p["w_mlp_down"][l].astype(BF16), p["g_mlp_post"][l:l + 1])
    return y.reshape(b, t, d)


def kernel(x_prompt, x_sample, rel_bias, g_mix_pre, w_in, lam_q1, lam_k1, lam_q2, lam_k2, g_attn_sub, lb_fwd, lb_bwd, g_hgrn_out, w_proj_a, w_proj_b, w_out, g_mix_post, g_mlp_pre, w_mlp_up, w_mlp_down, g_mlp_post):
    p = dict(rel_bias=rel_bias, g_mix_pre=g_mix_pre, w_in=w_in, lam_q1=lam_q1, lam_k1=lam_k1,
             lam_q2=lam_q2, lam_k2=lam_k2, g_attn_sub=g_attn_sub, lb_fwd=lb_fwd, lb_bwd=lb_bwd,
             g_hgrn_out=g_hgrn_out, w_proj_a=w_proj_a, w_proj_b=w_proj_b, w_out=w_out,
             g_mix_post=g_mix_post, g_mlp_pre=g_mlp_pre, w_mlp_up=w_mlp_up, w_mlp_down=w_mlp_down,
             g_mlp_post=g_mlp_post)

    def trunk(x):
        for l in range(g_mix_pre.shape[0]):
            x = _encoder_layer(x, l, p)
        return x

    return (trunk(x_prompt), trunk(x_sample))
```

```python
import functools
import math

import jax
import jax.numpy as jnp
from jax import lax
from jax.experimental import pallas as pl
from jax.experimental.pallas import tpu as pltpu

D_MODEL = 1024
HEADS = 8
HEAD_W = 128
QK_DIM = 64
NUM_BUCKETS = 32
D_FF = 4 * D_MODEL
EPS = 1e-6
CHUNK = 64
LANES = 128
LOG2E = 1.4426950408889634
VMEM_LIMIT = 56 * 1024 * 1024

F32 = jnp.float32
BF16 = jnp.bfloat16


def _cparams(sem):
    return pltpu.CompilerParams(dimension_semantics=sem, vmem_limit_bytes=VMEM_LIMIT)


def _rms(x, g):
    return x * lax.rsqrt(jnp.mean(x * x, axis=-1, keepdims=True) + EPS) * g


def _dot_nt(a, b):
    return lax.dot_general(a, b, (((1,), (1,)), ((), ())), preferred_element_type=F32)


def _dot_tn(a, b):
    return lax.dot_general(a, b, (((0,), (0,)), ((), ())), preferred_element_type=F32)


def _inproj_kernel(x_ref, g_ref, w_ref, cs_ref, o_ref, xn_ref):
    @pl.when(pl.program_id(1) == 0)
    def _():
        xn_ref[...] = _rms(x_ref[...], g_ref[...]).astype(BF16)

    acc = jnp.dot(xn_ref[...], w_ref[...], preferred_element_type=F32)
    o_ref[...] = (acc * cs_ref[...]).astype(o_ref.dtype)


def _inproj(x2, g, w, col_scale, out_dtype):
    n, d = x2.shape
    nout = w.shape[1]
    tm = min(1024, n)
    tn = 1024
    return pl.pallas_call(
        _inproj_kernel,
        grid=(n // tm, nout // tn),
        in_specs=[
            pl.BlockSpec((tm, d), lambda i, j: (i, 0)),
            pl.BlockSpec((1, d), lambda i, j: (0, 0)),
            pl.BlockSpec((d, tn), lambda i, j: (0, j)),
            pl.BlockSpec((1, tn), lambda i, j: (0, j)),
        ],
        out_specs=pl.BlockSpec((tm, tn), lambda i, j: (i, j)),
        out_shape=jax.ShapeDtypeStruct((n, nout), out_dtype),
        scratch_shapes=[pltpu.VMEM((tm, d), BF16)],
        compiler_params=_cparams(("parallel", "arbitrary")),
        name="inproj",
    )(x2, g, w, col_scale)


def _t5_bias_row(d, relb_ref, h):
    n = jnp.abs(d)
    n2 = n * n
    big = jnp.full(d.shape, 8, jnp.int32)
    for m in range(1, 8):
        big = big + (n2 >= 64 * 2 ** m).astype(jnp.int32)
    big = jnp.minimum(big, NUM_BUCKETS // 2 - 1)
    bucket = jnp.where(d > 0, NUM_BUCKETS // 2, 0) + jnp.where(n < 8, n, big)
    val = jnp.zeros(d.shape, F32)
    for b in range(NUM_BUCKETS):
        val = jnp.where(bucket == b, relb_ref[b, h], val)
    return val


def _attn_kernel(relb_ref, q_ref, k_ref, v_ref, gsub_ref, lq1_ref, lk1_ref, lq2_ref, lk2_ref,
                 o_ref, q2_ref, vext_ref, bias_ref, s_ref, mcur_ref, p_ref, alpha_ref, m_ref, acc_ref,
                 *, tq, tk, nk, lam_init):
    h = pl.program_id(1)
    qi = pl.program_id(2)
    t = nk * tk
    rb = LANES
    ntile = tk // LANES
    ratio = tk // tq

    @pl.when(qi == 0)
    def _():
        vext_ref[:, 0:HEAD_W] = v_ref[0]
        vext_ref[:, HEAD_W:2 * HEAD_W] = jnp.ones((t, HEAD_W), BF16)
        width = tq + tk
        col = lax.broadcasted_iota(jnp.int32, (1, width), 1)
        for idx in range(ratio + 2):
            delta = (idx - ratio) * tq
            row_bias = _t5_bias_row(delta + col - tq, relb_ref, h) * LOG2E
            tab = jnp.broadcast_to(row_bias, (tq, width))
            tab = pltpu.roll(tab, 0, 1, stride=1, stride_axis=0)
            bias_ref[idx] = tab[:, tq:tq + tk]

    q = q_ref[0]
    lane = lax.broadcasted_iota(jnp.int32, q.shape, 1)
    zero = jnp.zeros_like(q)
    q2_ref[0:tq, :] = jnp.where(lane < QK_DIM, q, zero)
    q2_ref[tq:2 * tq, :] = jnp.where(lane >= QK_DIM, q, zero)
    m_ref[...] = jnp.full(m_ref.shape, -jnp.inf, F32)
    acc_ref[...] = jnp.zeros(acc_ref.shape, F32)
    bias_left = relb_ref[NUM_BUCKETS // 2 - 1, h] * LOG2E
    bias_right = relb_ref[NUM_BUCKETS - 1, h] * LOG2E

    def row_max(x):
        mx = x[:, 0:LANES]
        for u in range(1, ntile):
            mx = jnp.maximum(mx, x[:, u * LANES:(u + 1) * LANES])
        return jnp.broadcast_to(jnp.max(mx, axis=1, keepdims=True), mx.shape)

    def scores(j, slot):
        r0 = pl.multiple_of(j * tk, tk)
        sv = _dot_nt(q2_ref[...], k_ref[0, pl.ds(r0, tk), :])
        s_ref[slot] = sv
        mcur_ref[slot] = row_max(sv)

    def add_band_bias(j, slot):
        idx = ratio * j - qi + ratio

        @pl.when(jnp.logical_and(idx >= 0, idx <= ratio + 1))
        def _():
            for half in range(2):
                for b in range(tq // rb):
                    rows = slice(half * tq + b * rb, half * tq + (b + 1) * rb)
                    sv = s_ref[slot, rows, :] + bias_ref[idx, b * rb:(b + 1) * rb, :]
                    s_ref[slot, rows, :] = sv
                    mcur_ref[slot, rows, :] = row_max(sv)

    def consume(j, slot):
        idx = ratio * j - qi + ratio
        c = jnp.where(idx < 0, bias_left, jnp.where(idx > ratio + 1, bias_right, 0.0))
        r0 = pl.multiple_of(j * tk, tk)
        for b in range(2 * tq // rb):
            rows = slice(b * rb, (b + 1) * rb)
            m_prev = m_ref[rows, :]
            m_new = jnp.maximum(m_prev, mcur_ref[slot, rows, :] + c)
            alpha_ref[rows, :] = jnp.exp2(m_prev - m_new)
            m_ref[rows, :] = m_new
            shift = m_new - c
            for u in range(ntile):
                cols = slice(u * LANES, (u + 1) * LANES)
                p_ref[rows, cols] = jnp.exp2(s_ref[slot, rows, cols] - shift).astype(BF16)
        pv = jnp.dot(p_ref[...], vext_ref[pl.ds(r0, tk), :], preferred_element_type=F32)
        alpha = alpha_ref[...]
        acc_ref[...] = jnp.concatenate([alpha, alpha], axis=1) * acc_ref[...] + pv

    def step(j, cur, issue_next):
        if issue_next:
            scores(j + 1, 1 - cur)
        consume(j, cur)
        if issue_next:
            add_band_bias(j + 1, 1 - cur)

    scores(0, 0)
    add_band_bias(0, 0)

    def pair(jj, carry):
        step(2 * jj, 0, True)
        step(2 * jj + 1, 1, True)
        return carry

    lax.fori_loop(0, nk // 2 - 1, pair, 0)
    step(nk - 2, 0, True)
    step(nk - 1, 1, False)

    lam = (jnp.exp(jnp.sum(lq1_ref[...] * lk1_ref[...], axis=1, keepdims=True))
           - jnp.exp(jnp.sum(lq2_ref[...] * lk2_ref[...], axis=1, keepdims=True))
           + lam_init)
    acc = acc_ref[...]
    o_all = acc[:, 0:HEAD_W] / acc[:, HEAD_W:2 * HEAD_W]
    o = o_all[0:tq, :] - lam * o_all[tq:2 * tq, :]
    o_ref[0] = (_rms(o, gsub_ref[...]) * (1.0 - lam_init)).astype(o_ref.dtype)


def _attention(qkv, rel_bias, g_sub, lq1, lk1, lq2, lk2, lam_init):
    b, t, _ = qkv.shape
    tq = min(512, t // 4)
    tk = 2 * tq
    nk = t // tk
    assert nk % 2 == 0 and tq >= 128
    small = pl.BlockSpec((1, QK_DIM), lambda bi, h, qi: (0, 0))
    kern = functools.partial(_attn_kernel, tq=tq, tk=tk, nk=nk, lam_init=lam_init)
    return pl.pallas_call(
        kern,
        grid=(b, HEADS, t // tq),
        in_specs=[
            pl.BlockSpec(memory_space=pltpu.SMEM),
            pl.BlockSpec((1, tq, HEAD_W), lambda bi, h, qi: (bi, qi, h)),
            pl.BlockSpec((1, t, HEAD_W), lambda bi, h, qi: (bi, 0, HEADS + h)),
            pl.BlockSpec((1, t, HEAD_W), lambda bi, h, qi: (bi, 0, 2 * HEADS + h)),
            pl.BlockSpec((1, HEAD_W), lambda bi, h, qi: (0, 0)),
            small, small, small, small,
        ],
        out_specs=pl.BlockSpec((1, tq, HEAD_W), lambda bi, h, qi: (bi, qi, h)),
        out_shape=jax.ShapeDtypeStruct((b, t, HEADS * HEAD_W), BF16),
        scratch_shapes=[
            pltpu.VMEM((2 * tq, HEAD_W), BF16),
            pltpu.VMEM((t, 2 * HEAD_W), BF16),
            pltpu.VMEM((tk // tq + 2, tq, tk), F32),
            pltpu.VMEM((2, 2 * tq, tk), F32),
            pltpu.VMEM((2, 2 * tq, LANES), F32),
            pltpu.VMEM((2 * tq, tk), BF16),
            pltpu.VMEM((2 * tq, LANES), F32),
            pltpu.VMEM((2 * tq, LANES), F32),
            pltpu.VMEM((2 * tq, 2 * HEAD_W), F32),
        ],
        compiler_params=_cparams(("parallel", "parallel", "arbitrary")),
        name="diff_attention",
    )(rel_bias, qkv, qkv, qkv, g_sub, lq1, lk1, lq2, lk2)


def _shift_rows(x, k):
    return pltpu.roll(x, k % x.shape[0], 0)


def _block_bcast(x, pos, c, rev):
    src = c if rev else c - 1
    v = jnp.where(pos == src, x, 0.0)
    k = 1
    while k < c:
        v = v + _shift_rows(v, k if rev else -k)
        k *= 2
    return v + _shift_rows(v, -c if rev else c)


def _hgrn_chunk(q, kk, g, v, st, rev):
    shape = q.shape
    row = lax.broadcasted_iota(jnp.int32, shape, 0)
    ti = lax.broadcasted_iota(jnp.int32, (CHUNK, CHUNK), 0)
    si = lax.broadcasted_iota(jnp.int32, (CHUNK, CHUNK), 1)
    qb = q.astype(BF16)
    kb = kk.astype(BF16)
    a = jnp.where(ti == si, _dot_nt(qb, kb), 0.0)
    state = g
    c = 1
    while c < CHUNK:
        pos = row & (2 * c - 1)
        is_q = (pos < c) if rev else (pos >= c)
        tp = _block_bcast(state, pos, c, rev)
        e = jnp.exp(jnp.where(is_q, state, tp - state))
        qf = jnp.where(is_q, q * e, 0.0).astype(BF16)
        kf = jnp.where(is_q, 0.0, kk * e).astype(BF16)
        shift = int(math.log2(2 * c))
        same = (ti >> shift) == (si >> shift)
        a = a + jnp.where(same, _dot_nt(qf, kf), 0.0)
        state = state + jnp.where(is_q, tp, 0.0)
        c *= 2
    edge = 0 if rev else CHUNK - 1
    tot = state[edge:edge + 1, :]
    qe = (q * jnp.exp(state)).astype(BF16)
    kd = (kk * jnp.exp(tot - state)).astype(BF16)
    vb = v.astype(BF16)
    o = _dot_nt(qe, st.astype(BF16)) + jnp.dot(a.astype(BF16), vb, preferred_element_type=F32)
    st_new = st * jnp.exp(tot) + _dot_tn(vb, kd)
    return o, st_new


def _hgrn_scan(q_ref, z_ref, i_ref, lb_ref, st_ref, emit, *, tb, rev):
    @pl.when(pl.program_id(2) == 0)
    def _():
        st_ref[...] = jnp.zeros(st_ref.shape, F32)

    lbx = lb_ref[...]
    ex = jnp.exp(lbx - jnp.max(lbx, axis=0, keepdims=True))
    lb = ex[0:1, :] / jnp.sum(ex, axis=0, keepdims=True)
    nchunk = tb // CHUNK

    def body(j, carry):
        jj = (nchunk - 1 - j) if rev else j
        r0 = pl.multiple_of(jj * CHUNK, CHUNK)
        f = lb + (1.0 - lb) * jax.nn.sigmoid(z_ref[0, pl.ds(r0, CHUNK), :])
        o, st_new = _hgrn_chunk(q_ref[0, pl.ds(r0, CHUNK), :], 1.0 - f, jnp.log(f),
                                i_ref[0, pl.ds(r0, CHUNK), :], st_ref[...], rev)
        st_ref[...] = st_new
        emit(r0, o)
        return carry

    lax.fori_loop(0, nchunk, body, 0)


def _hgrn_fwd_kernel(q_ref, z_ref, i_ref, lb_ref, o_ref, st_ref, *, tb):
    def emit(r0, o):
        o_ref[0, pl.ds(r0, CHUNK), :] = o

    _hgrn_scan(q_ref, z_ref, i_ref, lb_ref, st_ref, emit, tb=tb, rev=False)


def _hgrn_bwd_kernel(q_ref, z_ref, i_ref, lb_ref, of_ref, og_ref, gout_ref, o_ref, st_ref, *, tb):
    def emit(r0, o):
        rows = pl.ds(r0, CHUNK)
        y = _rms(o + of_ref[0, rows, :], gout_ref[...])
        o_ref[0, rows, :] = (y * jax.nn.silu(og_ref[0, rows, :])).astype(o_ref.dtype)

    _hgrn_scan(q_ref, z_ref, i_ref, lb_ref, st_ref, emit, tb=tb, rev=True)


def _hgrn2(proj, lb_fwd, lb_bwd, g_out):
    b, t, _ = proj.shape
    tb = min(1024, t)
    nb = t // tb

    def col(blk, rev):
        if rev:
            return pl.BlockSpec((1, tb, HEAD_W), lambda bi, h, ti: (bi, nb - 1 - ti, blk * HEADS + h))
        return pl.BlockSpec((1, tb, HEAD_W), lambda bi, h, ti: (bi, ti, blk * HEADS + h))

    lb_spec = pl.BlockSpec((lb_fwd.shape[0], HEAD_W), lambda bi, h, ti: (0, h))
    sem = _cparams(("parallel", "parallel", "arbitrary"))
    o_fwd = pl.pallas_call(
        functools.partial(_hgrn_fwd_kernel, tb=tb),
        grid=(b, HEADS, nb),
        in_specs=[col(0, False), col(1, False), col(3, False), lb_spec],
        out_specs=pl.BlockSpec((1, tb, HEAD_W), lambda bi, h, ti: (bi, ti, h)),
        out_shape=jax.ShapeDtypeStruct((b, t, HEADS * HEAD_W), F32),
        scratch_shapes=[pltpu.VMEM((HEAD_W, HEAD_W), F32)],
        compiler_params=sem,
        name="hgrn_fwd",
    )(proj, proj, proj, lb_fwd)
    rev_out = pl.BlockSpec((1, tb, HEAD_W), lambda bi, h, ti: (bi, nb - 1 - ti, h))
    return pl.pallas_call(
        functools.partial(_hgrn_bwd_kernel, tb=tb),
        grid=(b, HEADS, nb),
        in_specs=[col(0, True), col(2, True), col(3, True), lb_spec, rev_out, col(4, True),
                  pl.BlockSpec((1, HEAD_W), lambda bi, h, ti: (0, 0))],
        out_specs=rev_out,
        out_shape=jax.ShapeDtypeStruct((b, t, HEADS * HEAD_W), BF16),
        scratch_shapes=[pltpu.VMEM((HEAD_W, HEAD_W), F32)],
        compiler_params=sem,
        name="hgrn_bwd",
    )(proj, proj, proj, lb_bwd, o_fwd, proj, g_out)


def _merge_kernel(x_ref, oa_ref, ob_ref, ga_ref, gb_ref, wa_ref, wb_ref, wo_ref, g_ref, o_ref):
    pa = jnp.dot(oa_ref[...], wa_ref[...], preferred_element_type=F32)
    pb = jnp.dot(ob_ref[...], wb_ref[...], preferred_element_type=F32)
    merged = jax.nn.sigmoid(ga_ref[...]) * pa + jax.nn.sigmoid(gb_ref[...]) * pb
    t = jnp.dot(merged.astype(BF16), wo_ref[...], preferred_element_type=F32)
    o_ref[...] = x_ref[...] + _rms(t, g_ref[...])


def _merge(x2, oa, ob, proj2, wa, wb, wo, g):
    n, d = x2.shape
    tm = min(512, n)
    row = lambda blk: pl.BlockSpec((tm, d), lambda i: (i, blk))
    full = pl.BlockSpec((d, d), lambda i: (0, 0))
    return pl.pallas_call(
        _merge_kernel,
        grid=(n // tm,),
        in_specs=[row(0), row(0), row(0), row(5), row(6), full, full, full,
                  pl.BlockSpec((1, d), lambda i: (0, 0))],
        out_specs=row(0),
        out_shape=jax.ShapeDtypeStruct((n, d), F32),
        compiler_params=_cparams(("parallel",)),
        name="merge_out",
    )(x2, oa, ob, proj2, proj2, wa, wb, wo, g)


def _mlp_kernel(x_ref, gpre_ref, wu_ref, wd_ref, gpost_ref, o_ref, hn_ref, acc_ref):
    j = pl.program_id(1)

    @pl.when(j == 0)
    def _():
        hn_ref[...] = _rms(x_ref[...], gpre_ref[...]).astype(BF16)
        acc_ref[...] = jnp.zeros(acc_ref.shape, F32)

    u = jnp.square(jnp.maximum(jnp.dot(hn_ref[...], wu_ref[...], preferred_element_type=F32), 0.0))
    acc_ref[...] += jnp.dot(u.astype(BF16), wd_ref[...], preferred_element_type=F32)

    @pl.when(j == pl.num_programs(1) - 1)
    def _():
        o_ref[...] = x_ref[...] + _rms(acc_ref[...], gpost_ref[...])


def _mlp(x2, gpre, wu, wd, gpost):
    n, d = x2.shape
    tm = min(1024, n)
    tf = 1024
    vec = pl.BlockSpec((1, d), lambda i, j: (0, 0))
    return pl.pallas_call(
        _mlp_kernel,
        grid=(n // tm, D_FF // tf),
        in_specs=[pl.BlockSpec((tm, d), lambda i, j: (i, 0)), vec,
                  pl.BlockSpec((d, tf), lambda i, j: (0, j)),
                  pl.BlockSpec((tf, d), lambda i, j: (j, 0)), vec],
        out_specs=pl.BlockSpec((tm, d), lambda i, j: (i, 0)),
        out_shape=jax.ShapeDtypeStruct((n, d), F32),
        scratch_shapes=[pltpu.VMEM((tm, d), BF16), pltpu.VMEM((tm, d), F32)],
        compiler_params=_cparams(("parallel", "arbitrary")),
        name="mlp",
    )(x2, gpre, wu, wd, gpost)


def _encoder_layer(x, l, p):
    b, t, d = x.shape
    n = b * t
    x2 = x.reshape(n, d)
    n_attn = 3 * HEADS * HEAD_W
    w_in = p["w_in"][l]
    width = HEADS * HEAD_W
    attn_scale = jnp.concatenate([jnp.full((1, width), QK_DIM ** -0.5, F32),
                                  jnp.full((1, width), LOG2E, F32), jnp.ones((1, width), F32)], axis=1)
    qkv = _inproj(x2, p["g_mix_pre"][l:l + 1], w_in[:, :n_attn].astype(BF16), attn_scale, BF16)
    proj = _inproj(x2, p["g_mix_pre"][l:l + 1], w_in[:, n_attn:].astype(BF16),
                   jnp.ones((1, w_in.shape[1] - n_attn), F32), F32)
    lam_init = 0.8 - 0.6 * math.exp(-0.3 * l)
    o_a = _attention(qkv.reshape(b, t, n_attn), p["rel_bias"], p["g_attn_sub"][l:l + 1],
                     p["lam_q1"][l:l + 1], p["lam_k1"][l:l + 1], p["lam_q2"][l:l + 1],
                     p["lam_k2"][l:l + 1], lam_init)
    o_b = _hgrn2(proj.reshape(b, t, proj.shape[1]), p["lb_fwd"], p["lb_bwd"], p["g_hgrn_out"][l:l + 1])
    x1 = _merge(x2, o_a.reshape(n, d), o_b.reshape(n, d), proj,
                p["w_proj_a"][l].astype(BF16), p["w_proj_b"][l].astype(BF16),
                p["w_out"][l].astype(BF16), p["g_mix_post"][l:l + 1])
    y = _mlp(x1, p["g_mlp_pre"][l:l + 1], p["w_mlp_up"][l].astype(BF16),
             p["w_mlp_down"][l].astype(BF16), p["g_mlp_post"][l:l + 1])
    return y.reshape(b, t, d)


def kernel(x_prompt, x_sample, rel_bias, g_mix_pre, w_in, lam_q1, lam_k1, lam_q2, lam_k2, g_attn_sub, lb_fwd, lb_bwd, g_hgrn_out, w_proj_a, w_proj_b, w_out, g_mix_post, g_mlp_pre, w_mlp_up, w_mlp_down, g_mlp_post):
    p = dict(rel_bias=rel_bias, g_mix_pre=g_mix_pre, w_in=w_in, lam_q1=lam_q1, lam_k1=lam_k1,
             lam_q2=lam_q2, lam_k2=lam_k2, g_attn_sub=g_attn_sub, lb_fwd=lb_fwd, lb_bwd=lb_bwd,
             g_hgrn_out=g_hgrn_out, w_proj_a=w_proj_a, w_proj_b=w_proj_b, w_out=w_out,
             g_mix_post=g_mix_post, g_mlp_pre=g_mlp_pre, w_mlp_up=w_mlp_up, w_mlp_down=w_mlp_down,
             g_mlp_post=g_mlp_post)

    def trunk(x):
        for l in range(g_mix_pre.shape[0]):
            x = _encoder_layer(x, l, p)
        return x

    return (trunk(x_prompt), trunk(x_sample))
```

```python
import functools
import math

import jax
import jax.numpy as jnp
from jax import lax
from jax.experimental import pallas as pl
from jax.experimental.pallas import tpu as pltpu

D_MODEL = 1024
HEADS = 8
HEAD_W = 128
QK_DIM = 64
NUM_BUCKETS = 32
D_FF = 4 * D_MODEL
EPS = 1e-6
CHUNK = 64
LANES = 128
LOG2E = 1.4426950408889634
SAFE_RANGE = 60.0
VMEM_LIMIT = 56 * 1024 * 1024

F32 = jnp.float32
BF16 = jnp.bfloat16


def _cparams(sem):
    return pltpu.CompilerParams(dimension_semantics=sem, vmem_limit_bytes=VMEM_LIMIT)


def _rms(x, g):
    return x * lax.rsqrt(jnp.mean(x * x, axis=-1, keepdims=True) + EPS) * g


def _dot_nt(a, b):
    return lax.dot_general(a, b, (((1,), (1,)), ((), ())), preferred_element_type=F32)


def _dot_tn(a, b):
    return lax.dot_general(a, b, (((0,), (0,)), ((), ())), preferred_element_type=F32)


def _inproj_kernel(x_ref, g_ref, w_ref, cs_ref, o_ref, xn_ref):
    @pl.when(pl.program_id(1) == 0)
    def _():
        xn_ref[...] = _rms(x_ref[...], g_ref[...]).astype(BF16)

    acc = jnp.dot(xn_ref[...], w_ref[...], preferred_element_type=F32)
    o_ref[...] = (acc * cs_ref[...]).astype(o_ref.dtype)


def _inproj(x2, g, w, col_scale, out_dtype):
    n, d = x2.shape
    nout = w.shape[1]
    tm = min(1024, n)
    tn = 1024
    return pl.pallas_call(
        _inproj_kernel,
        grid=(n // tm, nout // tn),
        in_specs=[
            pl.BlockSpec((tm, d), lambda i, j: (i, 0)),
            pl.BlockSpec((1, d), lambda i, j: (0, 0)),
            pl.BlockSpec((d, tn), lambda i, j: (0, j)),
            pl.BlockSpec((1, tn), lambda i, j: (0, j)),
        ],
        out_specs=pl.BlockSpec((tm, tn), lambda i, j: (i, j)),
        out_shape=jax.ShapeDtypeStruct((n, nout), out_dtype),
        scratch_shapes=[pltpu.VMEM((tm, d), BF16)],
        compiler_params=_cparams(("parallel", "arbitrary")),
        name="inproj",
    )(x2, g, w, col_scale)


def _t5_bias_row(d, relb_ref, h):
    n = jnp.abs(d)
    n2 = n * n
    big = jnp.full(d.shape, 8, jnp.int32)
    for m in range(1, 8):
        big = big + (n2 >= 64 * 2 ** m).astype(jnp.int32)
    big = jnp.minimum(big, NUM_BUCKETS // 2 - 1)
    bucket = jnp.where(d > 0, NUM_BUCKETS // 2, 0) + jnp.where(n < 8, n, big)
    val = jnp.zeros(d.shape, F32)
    for b in range(NUM_BUCKETS):
        val = jnp.where(bucket == b, relb_ref[b, h], val)
    return val


def _attn_kernel(relb_ref, q_ref, k_ref, v_ref, gsub_ref, lq1_ref, lk1_ref, lq2_ref, lk2_ref,
                 o_ref, q2_ref, vext_ref, bias_ref, s_ref, mcur_ref, alpha_ref, m_ref, acc_ref,
                 *, tq, tk, nk, lam_init):
    h = pl.program_id(1)
    qi = pl.program_id(2)
    t = nk * tk
    rb = LANES
    ntile = tk // LANES
    ratio = tk // tq
    n_band = ratio + 2

    @pl.when(qi == 0)
    def _():
        vext_ref[:, 0:HEAD_W] = v_ref[0]
        vext_ref[:, HEAD_W:2 * HEAD_W] = jnp.ones((t, HEAD_W), BF16)
        bias_ref[0] = jnp.full((tq, tk), relb_ref[NUM_BUCKETS // 2 - 1, h] * LOG2E, F32)
        bias_ref[n_band + 1] = jnp.full((tq, tk), relb_ref[NUM_BUCKETS - 1, h] * LOG2E, F32)
        width = tq + tk
        col = lax.broadcasted_iota(jnp.int32, (1, width), 1)
        for idx in range(n_band):
            delta = (idx - ratio) * tq
            row_bias = _t5_bias_row(delta + col - tq, relb_ref, h) * LOG2E
            tab = jnp.broadcast_to(row_bias, (tq, width))
            tab = pltpu.roll(tab, 0, 1, stride=1, stride_axis=0)
            bias_ref[idx + 1] = tab[:, tq:tq + tk]

    q = q_ref[0]
    lane = lax.broadcasted_iota(jnp.int32, q.shape, 1)
    zero = jnp.zeros_like(q)
    q2_ref[0:tq, :] = jnp.where(lane < QK_DIM, q, zero)
    q2_ref[tq:2 * tq, :] = jnp.where(lane >= QK_DIM, q, zero)
    m_ref[...] = jnp.full(m_ref.shape, -jnp.inf, F32)
    acc_ref[...] = jnp.zeros(acc_ref.shape, F32)

    def row_max(x):
        mx = x[:, 0:LANES]
        for u in range(1, ntile):
            mx = jnp.maximum(mx, x[:, u * LANES:(u + 1) * LANES])
        return jnp.broadcast_to(jnp.max(mx, axis=1, keepdims=True), mx.shape)

    def scores(j, slot):
        tile = jnp.clip(ratio * j - qi + ratio, -1, n_band) + 1
        bias = bias_ref[tile]
        r0 = pl.multiple_of(j * tk, tk)
        sv = _dot_nt(q2_ref[...], k_ref[0, pl.ds(r0, tk), :])
        sv = sv + jnp.concatenate([bias, bias], axis=0)
        s_ref[slot] = sv
        mcur_ref[slot] = row_max(sv)

    def consume(j, slot):
        blocks = []
        for b in range(2 * tq // rb):
            rows = slice(b * rb, (b + 1) * rb)
            m_prev = m_ref[rows, :]
            m_new = jnp.maximum(m_prev, mcur_ref[slot, rows, :])
            alpha_ref[rows, :] = jnp.exp2(m_prev - m_new)
            m_ref[rows, :] = m_new
            blocks.append(jnp.concatenate(
                [jnp.exp2(s_ref[slot, rows, u * LANES:(u + 1) * LANES] - m_new).astype(BF16)
                 for u in range(ntile)], axis=1))
        p = jnp.concatenate(blocks, axis=0)
        r0 = pl.multiple_of(j * tk, tk)
        pv = jnp.dot(p, vext_ref[pl.ds(r0, tk), :], preferred_element_type=F32)
        alpha = alpha_ref[...]
        acc_ref[...] = jnp.concatenate([alpha, alpha], axis=1) * acc_ref[...] + pv

    scores(0, 0)

    def pair(jj, carry):
        scores(2 * jj + 1, 1)
        consume(2 * jj, 0)
        scores(2 * jj + 2, 0)
        consume(2 * jj + 1, 1)
        return carry

    lax.fori_loop(0, nk // 2 - 1, pair, 0)
    scores(nk - 1, 1)
    consume(nk - 2, 0)
    consume(nk - 1, 1)

    lam = (jnp.exp(jnp.sum(lq1_ref[...] * lk1_ref[...], axis=1, keepdims=True))
           - jnp.exp(jnp.sum(lq2_ref[...] * lk2_ref[...], axis=1, keepdims=True))
           + lam_init)
    acc = acc_ref[...]
    o_all = acc[:, 0:HEAD_W] / acc[:, HEAD_W:2 * HEAD_W]
    o = o_all[0:tq, :] - lam * o_all[tq:2 * tq, :]
    o_ref[0] = (_rms(o, gsub_ref[...]) * (1.0 - lam_init)).astype(o_ref.dtype)


def _attention(qkv, rel_bias, g_sub, lq1, lk1, lq2, lk2, lam_init):
    b, t, _ = qkv.shape
    tq = min(512, t // 4)
    tk = 2 * tq
    nk = t // tk
    assert tq >= 128
    small = pl.BlockSpec((1, QK_DIM), lambda bi, h, qi: (0, 0))
    kern = functools.partial(_attn_kernel, tq=tq, tk=tk, nk=nk, lam_init=lam_init)
    return pl.pallas_call(
        kern,
        grid=(b, HEADS, t // tq),
        in_specs=[
            pl.BlockSpec(memory_space=pltpu.SMEM),
            pl.BlockSpec((1, tq, HEAD_W), lambda bi, h, qi: (bi, qi, h)),
            pl.BlockSpec((1, t, HEAD_W), lambda bi, h, qi: (bi, 0, HEADS + h)),
            pl.BlockSpec((1, t, HEAD_W), lambda bi, h, qi: (bi, 0, 2 * HEADS + h)),
            pl.BlockSpec((1, HEAD_W), lambda bi, h, qi: (0, 0)),
            small, small, small, small,
        ],
        out_specs=pl.BlockSpec((1, tq, HEAD_W), lambda bi, h, qi: (bi, qi, h)),
        out_shape=jax.ShapeDtypeStruct((b, t, HEADS * HEAD_W), BF16),
        scratch_shapes=[
            pltpu.VMEM((2 * tq, HEAD_W), BF16),
            pltpu.VMEM((t, 2 * HEAD_W), BF16),
            pltpu.VMEM((tk // tq + 4, tq, tk), F32),
            pltpu.VMEM((2, 2 * tq, tk), F32),
            pltpu.VMEM((2, 2 * tq, LANES), F32),
            pltpu.VMEM((2 * tq, LANES), F32),
            pltpu.VMEM((2 * tq, LANES), F32),
            pltpu.VMEM((2 * tq, 2 * HEAD_W), F32),
        ],
        compiler_params=_cparams(("parallel", "parallel", "arbitrary")),
        name="diff_attention",
    )(rel_bias, qkv, qkv, qkv, g_sub, lq1, lk1, lq2, lk2)


def _shift_rows(x, k):
    return pltpu.roll(x, k % x.shape[0], 0)


def _block_bcast(x, pos, c, rev):
    src = c if rev else c - 1
    v = jnp.where(pos == src, x, 0.0)
    k = 1
    while k < c:
        v = v + _shift_rows(v, k if rev else -k)
        k *= 2
    return v + _shift_rows(v, -c if rev else c)


def _hgrn_chunk(q, kk, g, v, st, rev):
    shape = q.shape
    row = lax.broadcasted_iota(jnp.int32, shape, 0)
    ti = lax.broadcasted_iota(jnp.int32, (CHUNK, CHUNK), 0)
    si = lax.broadcasted_iota(jnp.int32, (CHUNK, CHUNK), 1)
    qb = q.astype(BF16)
    kb = kk.astype(BF16)
    a = jnp.where(ti == si, _dot_nt(qb, kb), 0.0)
    state = g
    c = 1
    while c < CHUNK:
        pos = row & (2 * c - 1)
        is_q = (pos < c) if rev else (pos >= c)
        tp = _block_bcast(state, pos, c, rev)
        e = jnp.exp(jnp.where(is_q, state, tp - state))
        qf = jnp.where(is_q, q * e, 0.0).astype(BF16)
        kf = jnp.where(is_q, 0.0, kk * e).astype(BF16)
        shift = int(math.log2(2 * c))
        same = (ti >> shift) == (si >> shift)
        a = a + jnp.where(same, _dot_nt(qf, kf), 0.0)
        state = state + jnp.where(is_q, tp, 0.0)
        c *= 2
    edge = 0 if rev else CHUNK - 1
    tot = state[edge:edge + 1, :]
    qe = (q * jnp.exp(state)).astype(BF16)
    kd = (kk * jnp.exp(tot - state)).astype(BF16)
    vb = v.astype(BF16)
    o = _dot_nt(qe, st.astype(BF16)) + jnp.dot(a.astype(BF16), vb, preferred_element_type=F32)
    st_new = st * jnp.exp(tot) + _dot_tn(vb, kd)
    return o, st_new


def _hgrn_chunk_midnorm(q, kk, cs, v, st, rev):
    half = CHUNK // 2
    mid = half if rev else half - 1
    edge = 0 if rev else CHUNK - 1
    ti = lax.broadcasted_iota(jnp.int32, (CHUNK, CHUNK), 0)
    si = lax.broadcasted_iota(jnp.int32, (CHUNK, CHUNK), 1)
    m = cs[mid:mid + 1, :]
    tot = cs[edge:edge + 1, :]
    qn = q * jnp.exp(cs - m)
    kn = kk * jnp.exp(m - cs)
    a = _dot_nt(qn.astype(BF16), kn.astype(BF16))
    a = jnp.where((si >= ti) if rev else (si <= ti), a, 0.0)
    qe = (qn * jnp.exp(m)).astype(BF16)
    kd = (kn * jnp.exp(tot - m)).astype(BF16)
    vb = v.astype(BF16)
    o = _dot_nt(qe, st.astype(BF16)) + jnp.dot(a.astype(BF16), vb, preferred_element_type=F32)
    st_new = st * jnp.exp(tot) + _dot_tn(vb, kd)
    return o, st_new


def _hgrn_scan(q_ref, z_ref, i_ref, lb_ref, st_ref, kk_ref, g_ref, cs_ref, emit, *, tb, rev):
    @pl.when(pl.program_id(2) == 0)
    def _():
        st_ref[...] = jnp.zeros(st_ref.shape, F32)

    lbx = lb_ref[...]
    ex = jnp.exp(lbx - jnp.max(lbx, axis=0, keepdims=True))
    lb = ex[0:1, :] / jnp.sum(ex, axis=0, keepdims=True)
    nchunk = tb // CHUNK

    f = lb + (1.0 - lb) * jax.nn.sigmoid(z_ref[0])
    g = jnp.log(f)
    kk_ref[...] = 1.0 - f
    g_ref[...] = g
    pos = lax.broadcasted_iota(jnp.int32, g.shape, 0) & (CHUNK - 1)
    cs = g
    k = 1
    while k < CHUNK:
        if rev:
            cs = cs + jnp.where(pos < CHUNK - k, _shift_rows(cs, -k), 0.0)
        else:
            cs = cs + jnp.where(pos >= k, _shift_rows(cs, k), 0.0)
        k *= 2
    cs_ref[...] = cs
    half = CHUNK // 2
    mid = half if rev else half - 1
    edge = 0 if rev else CHUNK - 1
    cs3 = cs.reshape(nchunk, CHUNK, cs.shape[-1])
    spread = jnp.max(jnp.maximum(-cs3[:, mid, :], cs3[:, mid, :] - cs3[:, edge, :]))
    safe = spread <= SAFE_RANGE

    def run(chunk_fn, aux_ref, unroll):
        def body(j, carry):
            jj = (nchunk - 1 - j) if rev else j
            rows = pl.ds(pl.multiple_of(jj * CHUNK, CHUNK), CHUNK)
            o, st_new = chunk_fn(q_ref[0, rows, :], kk_ref[rows, :], aux_ref[rows, :],
                                 i_ref[0, rows, :], st_ref[...], rev)
            st_ref[...] = st_new
            emit(rows, o)
            return carry

        lax.fori_loop(0, nchunk, body, 0, unroll=unroll)

    @pl.when(safe)
    def _():
        run(_hgrn_chunk_midnorm, cs_ref, min(8, nchunk))

    @pl.when(jnp.logical_not(safe))
    def _():
        run(_hgrn_chunk, g_ref, 1)


def _hgrn_fwd_kernel(q_ref, z_ref, i_ref, lb_ref, o_ref, st_ref, kk_ref, g_ref, cs_ref, *, tb):
    def emit(rows, o):
        o_ref[0, rows, :] = o

    _hgrn_scan(q_ref, z_ref, i_ref, lb_ref, st_ref, kk_ref, g_ref, cs_ref, emit, tb=tb, rev=False)


def _hgrn_bwd_kernel(q_ref, z_ref, i_ref, lb_ref, of_ref, og_ref, gout_ref, o_ref, st_ref, kk_ref,
                     g_ref, cs_ref, *, tb):
    def emit(rows, o):
        y = _rms(o + of_ref[0, rows, :], gout_ref[...])
        o_ref[0, rows, :] = (y * jax.nn.silu(og_ref[0, rows, :])).astype(o_ref.dtype)

    _hgrn_scan(q_ref, z_ref, i_ref, lb_ref, st_ref, kk_ref, g_ref, cs_ref, emit, tb=tb, rev=True)


def _hgrn2(proj, lb_fwd, lb_bwd, g_out):
    b, t, _ = proj.shape
    tb = min(1024, t)
    nb = t // tb

    def col(blk, rev):
        if rev:
            return pl.BlockSpec((1, tb, HEAD_W), lambda bi, h, ti: (bi, nb - 1 - ti, blk * HEADS + h))
        return pl.BlockSpec((1, tb, HEAD_W), lambda bi, h, ti: (bi, ti, blk * HEADS + h))

    lb_spec = pl.BlockSpec((lb_fwd.shape[0], HEAD_W), lambda bi, h, ti: (0, h))
    sem = _cparams(("parallel", "parallel", "arbitrary"))
    scratch = [pltpu.VMEM((HEAD_W, HEAD_W), F32)] + [pltpu.VMEM((tb, HEAD_W), F32)] * 3
    o_fwd = pl.pallas_call(
        functools.partial(_hgrn_fwd_kernel, tb=tb),
        grid=(b, HEADS, nb),
        in_specs=[col(0, False), col(1, False), col(3, False), lb_spec],
        out_specs=pl.BlockSpec((1, tb, HEAD_W), lambda bi, h, ti: (bi, ti, h)),
        out_shape=jax.ShapeDtypeStruct((b, t, HEADS * HEAD_W), F32),
        scratch_shapes=scratch,
        compiler_params=sem,
        name="hgrn_fwd",
    )(proj, proj, proj, lb_fwd)
    rev_out = pl.BlockSpec((1, tb, HEAD_W), lambda bi, h, ti: (bi, nb - 1 - ti, h))
    return pl.pallas_call(
        functools.partial(_hgrn_bwd_kernel, tb=tb),
        grid=(b, HEADS, nb),
        in_specs=[col(0, True), col(2, True), col(3, True), lb_spec, rev_out, col(4, True),
                  pl.BlockSpec((1, HEAD_W), lambda bi, h, ti: (0, 0))],
        out_specs=rev_out,
        out_shape=jax.ShapeDtypeStruct((b, t, HEADS * HEAD_W), BF16),
        scratch_shapes=scratch,
        compiler_params=sem,
        name="hgrn_bwd",
    )(proj, proj, proj, lb_bwd, o_fwd, proj, g_out)


def _merge_kernel(x_ref, oa_ref, ob_ref, ga_ref, gb_ref, wa_ref, wb_ref, wo_ref, g_ref, o_ref):
    pa = jnp.dot(oa_ref[...], wa_ref[...], preferred_element_type=F32)
    pb = jnp.dot(ob_ref[...], wb_ref[...], preferred_element_type=F32)
    merged = jax.nn.sigmoid(ga_ref[...]) * pa + jax.nn.sigmoid(gb_ref[...]) * pb
    t = jnp.dot(merged.astype(BF16), wo_ref[...], preferred_element_type=F32)
    o_ref[...] = x_ref[...] + _rms(t, g_ref[...])


def _merge(x2, oa, ob, proj2, wa, wb, wo, g):
    n, d = x2.shape
    tm = min(512, n)
    row = lambda blk: pl.BlockSpec((tm, d), lambda i: (i, blk))
    full = pl.BlockSpec((d, d), lambda i: (0, 0))
    return pl.pallas_call(
        _merge_kernel,
        grid=(n // tm,),
        in_specs=[row(0), row(0), row(0), row(5), row(6), full, full, full,
                  pl.BlockSpec((1, d), lambda i: (0, 0))],
        out_specs=row(0),
        out_shape=jax.ShapeDtypeStruct((n, d), F32),
        compiler_params=_cparams(("parallel",)),
        name="merge_out",
    )(x2, oa, ob, proj2, proj2, wa, wb, wo, g)


def _mlp_kernel(x_ref, gpre_ref, wu_ref, wd_ref, gpost_ref, o_ref, hn_ref, acc_ref):
    j = pl.program_id(1)

    @pl.when(j == 0)
    def _():
        hn_ref[...] = _rms(x_ref[...], gpre_ref[...]).astype(BF16)
        acc_ref[...] = jnp.zeros(acc_ref.shape, F32)

    u = jnp.square(jnp.maximum(jnp.dot(hn_ref[...], wu_ref[...], preferred_element_type=F32), 0.0))
    acc_ref[...] += jnp.dot(u.astype(BF16), wd_ref[...], preferred_element_type=F32)

    @pl.when(j == pl.num_programs(1) - 1)
    def _():
        o_ref[...] = x_ref[...] + _rms(acc_ref[...], gpost_ref[...])


def _mlp(x2, gpre, wu, wd, gpost):
    n, d = x2.shape
    tm = min(1024, n)
    tf = 1024
    vec = pl.BlockSpec((1, d), lambda i, j: (0, 0))
    return pl.pallas_call(
        _mlp_kernel,
        grid=(n // tm, D_FF // tf),
        in_specs=[pl.BlockSpec((tm, d), lambda i, j: (i, 0)), vec,
                  pl.BlockSpec((d, tf), lambda i, j: (0, j)),
                  pl.BlockSpec((tf, d), lambda i, j: (j, 0)), vec],
        out_specs=pl.BlockSpec((tm, d), lambda i, j: (i, 0)),
        out_shape=jax.ShapeDtypeStruct((n, d), F32),
        scratch_shapes=[pltpu.VMEM((tm, d), BF16), pltpu.VMEM((tm, d), F32)],
        compiler_params=_cparams(("parallel", "arbitrary")),
        name="mlp",
    )(x2, gpre, wu, wd, gpost)


def _encoder_layer(x, l, p):
    b, t, d = x.shape
    n = b * t
    x2 = x.reshape(n, d)
    n_attn = 3 * HEADS * HEAD_W
    w_in = p["w_in"][l]
    width = HEADS * HEAD_W
    attn_scale = jnp.concatenate([jnp.full((1, width), QK_DIM ** -0.5, F32),
                                  jnp.full((1, width), LOG2E, F32), jnp.ones((1, width), F32)], axis=1)
    qkv = _inproj(x2, p["g_mix_pre"][l:l + 1], w_in[:, :n_attn].astype(BF16), attn_scale, BF16)
    proj = _inproj(x2, p["g_mix_pre"][l:l + 1], w_in[:, n_attn:].astype(BF16),
                   jnp.ones((1, w_in.shape[1] - n_attn), F32), F32)
    lam_init = 0.8 - 0.6 * math.exp(-0.3 * l)
    o_a = _attention(qkv.reshape(b, t, n_attn), p["rel_bias"], p["g_attn_sub"][l:l + 1],
                     p["lam_q1"][l:l + 1], p["lam_k1"][l:l + 1], p["lam_q2"][l:l + 1],
                     p["lam_k2"][l:l + 1], lam_init)
    o_b = _hgrn2(proj.reshape(b, t, proj.shape[1]), p["lb_fwd"], p["lb_bwd"], p["g_hgrn_out"][l:l + 1])
    x1 = _merge(x2, o_a.reshape(n, d), o_b.reshape(n, d), proj,
                p["w_proj_a"][l].astype(BF16), p["w_proj_b"][l].astype(BF16),
                p["w_out"][l].astype(BF16), p["g_mix_post"][l:l + 1])
    y = _mlp(x1, p["g_mlp_pre"][l:l + 1], p["w_mlp_up"][l].astype(BF16),
             p["w_mlp_down"][l].astype(BF16), p["g_mlp_post"][l:l + 1])
    return y.reshape(b, t, d)


def kernel(x_prompt, x_sample, rel_bias, g_mix_pre, w_in, lam_q1, lam_k1, lam_q2, lam_k2, g_attn_sub, lb_fwd, lb_bwd, g_hgrn_out, w_proj_a, w_proj_b, w_out, g_mix_post, g_mlp_pre, w_mlp_up, w_mlp_down, g_mlp_post):
    p = dict(rel_bias=rel_bias, g_mix_pre=g_mix_pre, w_in=w_in, lam_q1=lam_q1, lam_k1=lam_k1,
             lam_q2=lam_q2, lam_k2=lam_k2, g_attn_sub=g_attn_sub, lb_fwd=lb_fwd, lb_bwd=lb_bwd,
             g_hgrn_out=g_hgrn_out, w_proj_a=w_proj_a, w_proj_b=w_proj_b, w_out=w_out,
             g_mix_post=g_mix_post, g_mlp_pre=g_mlp_pre, w_mlp_up=w_mlp_up, w_mlp_down=w_mlp_down,
             g_mlp_post=g_mlp_post)

    def trunk(x):
        for l in range(g_mix_pre.shape[0]):
            x = _encoder_layer(x, l, p)
        return x

    return (trunk(x_prompt), trunk(x_sample))
```

```python
import functools
import math

import jax
import jax.numpy as jnp
from jax import lax
from jax.experimental import pallas as pl
from jax.experimental.pallas import tpu as pltpu

D_MODEL = 1024
HEADS = 8
HEAD_W = 128
QK_DIM = 64
NUM_BUCKETS = 32
D_FF = 4 * D_MODEL
EPS = 1e-6
CHUNK = 64
LANES = 128
LOG2E = 1.4426950408889634
SAFE_RANGE = 60.0
VMEM_LIMIT = 56 * 1024 * 1024

F32 = jnp.float32
BF16 = jnp.bfloat16


def _cparams(sem):
    return pltpu.CompilerParams(dimension_semantics=sem, vmem_limit_bytes=VMEM_LIMIT)


def _rms(x, g):
    return x * lax.rsqrt(jnp.mean(x * x, axis=-1, keepdims=True) + EPS) * g


def _dot_nt(a, b):
    return lax.dot_general(a, b, (((1,), (1,)), ((), ())), preferred_element_type=F32)


def _dot_tn(a, b):
    return lax.dot_general(a, b, (((0,), (0,)), ((), ())), preferred_element_type=F32)


def _inproj_kernel(x_ref, g_ref, w_ref, cs_ref, o_ref, xn_ref):
    @pl.when(pl.program_id(1) == 0)
    def _():
        xn_ref[...] = _rms(x_ref[...], g_ref[...]).astype(BF16)

    acc = jnp.dot(xn_ref[...], w_ref[...], preferred_element_type=F32)
    o_ref[...] = (acc * cs_ref[...]).astype(o_ref.dtype)


def _inproj(x2, g, w, col_scale, out_dtype):
    n, d = x2.shape
    nout = w.shape[1]
    tm = min(1024, n)
    tn = 1024
    return pl.pallas_call(
        _inproj_kernel,
        grid=(n // tm, nout // tn),
        in_specs=[
            pl.BlockSpec((tm, d), lambda i, j: (i, 0)),
            pl.BlockSpec((1, d), lambda i, j: (0, 0)),
            pl.BlockSpec((d, tn), lambda i, j: (0, j)),
            pl.BlockSpec((1, tn), lambda i, j: (0, j)),
        ],
        out_specs=pl.BlockSpec((tm, tn), lambda i, j: (i, j)),
        out_shape=jax.ShapeDtypeStruct((n, nout), out_dtype),
        scratch_shapes=[pltpu.VMEM((tm, d), BF16)],
        compiler_params=_cparams(("parallel", "arbitrary")),
        name="inproj",
    )(x2, g, w, col_scale)


def _t5_bias_row(d, relb_ref, h):
    n = jnp.abs(d)
    n2 = n * n
    big = jnp.full(d.shape, 8, jnp.int32)
    for m in range(1, 8):
        big = big + (n2 >= 64 * 2 ** m).astype(jnp.int32)
    big = jnp.minimum(big, NUM_BUCKETS // 2 - 1)
    bucket = jnp.where(d > 0, NUM_BUCKETS // 2, 0) + jnp.where(n < 8, n, big)
    val = jnp.zeros(d.shape, F32)
    for b in range(NUM_BUCKETS):
        val = jnp.where(bucket == b, relb_ref[b, h], val)
    return val


def _attn_kernel(relb_ref, q_ref, k_ref, v_ref, gsub_ref, lq1_ref, lk1_ref, lq2_ref, lk2_ref,
                 o_ref, q2_ref, vext_ref, bias_ref, s_ref, mcur_ref, alpha_ref, m_ref, acc_ref,
                 *, tq, tk, nk, lam_init):
    h = pl.program_id(1)
    qi = pl.program_id(2)
    t = nk * tk
    rb = LANES
    ntile = tk // LANES
    ratio = tk // tq
    n_band = ratio + 2

    @pl.when(qi == 0)
    def _():
        vext_ref[:, 0:HEAD_W] = v_ref[0]
        vext_ref[:, HEAD_W:2 * HEAD_W] = jnp.ones((t, HEAD_W), BF16)
        bias_ref[0] = jnp.full((tq, tk), relb_ref[NUM_BUCKETS // 2 - 1, h] * LOG2E, F32)
        bias_ref[n_band + 1] = jnp.full((tq, tk), relb_ref[NUM_BUCKETS - 1, h] * LOG2E, F32)
        width = tq + tk
        col = lax.broadcasted_iota(jnp.int32, (1, width), 1)
        for idx in range(n_band):
            delta = (idx - ratio) * tq
            row_bias = _t5_bias_row(delta + col - tq, relb_ref, h) * LOG2E
            tab = jnp.broadcast_to(row_bias, (tq, width))
            tab = pltpu.roll(tab, 0, 1, stride=1, stride_axis=0)
            bias_ref[idx + 1] = tab[:, tq:tq + tk]

    q = q_ref[0]
    lane = lax.broadcasted_iota(jnp.int32, q.shape, 1)
    zero = jnp.zeros_like(q)
    q2_ref[0:tq, :] = jnp.where(lane < QK_DIM, q, zero)
    q2_ref[tq:2 * tq, :] = jnp.where(lane >= QK_DIM, q, zero)
    m_ref[...] = jnp.full(m_ref.shape, -jnp.inf, F32)
    acc_ref[...] = jnp.zeros(acc_ref.shape, F32)

    def row_max(x):
        mx = x[:, 0:LANES]
        for u in range(1, ntile):
            mx = jnp.maximum(mx, x[:, u * LANES:(u + 1) * LANES])
        return jnp.broadcast_to(jnp.max(mx, axis=1, keepdims=True), mx.shape)

    def scores(j, slot):
        tile = jnp.clip(ratio * j - qi + ratio, -1, n_band) + 1
        bias = bias_ref[tile]
        r0 = pl.multiple_of(j * tk, tk)
        sv = _dot_nt(q2_ref[...], k_ref[0, pl.ds(r0, tk), :])
        sv = sv + jnp.concatenate([bias, bias], axis=0)
        s_ref[slot] = sv
        mcur_ref[slot] = row_max(sv)

    def consume(j, slot):
        blocks = []
        for b in range(2 * tq // rb):
            rows = slice(b * rb, (b + 1) * rb)
            m_prev = m_ref[rows, :]
            m_new = jnp.maximum(m_prev, mcur_ref[slot, rows, :])
            alpha_ref[rows, :] = jnp.exp2(m_prev - m_new)
            m_ref[rows, :] = m_new
            blocks.append(jnp.concatenate(
                [jnp.exp2(s_ref[slot, rows, u * LANES:(u + 1) * LANES] - m_new).astype(BF16)
                 for u in range(ntile)], axis=1))
        p = jnp.concatenate(blocks, axis=0)
        r0 = pl.multiple_of(j * tk, tk)
        pv = jnp.dot(p, vext_ref[pl.ds(r0, tk), :], preferred_element_type=F32)
        alpha = alpha_ref[...]
        acc_ref[...] = jnp.concatenate([alpha, alpha], axis=1) * acc_ref[...] + pv

    scores(0, 0)

    def pair(jj, carry):
        scores(2 * jj + 1, 1)
        consume(2 * jj, 0)
        scores(2 * jj + 2, 0)
        consume(2 * jj + 1, 1)
        return carry

    lax.fori_loop(0, nk // 2 - 1, pair, 0)
    scores(nk - 1, 1)
    consume(nk - 2, 0)
    consume(nk - 1, 1)

    lam = (jnp.exp(jnp.sum(lq1_ref[...] * lk1_ref[...], axis=1, keepdims=True))
           - jnp.exp(jnp.sum(lq2_ref[...] * lk2_ref[...], axis=1, keepdims=True))
           + lam_init)
    acc = acc_ref[...]
    o_all = acc[:, 0:HEAD_W] / acc[:, HEAD_W:2 * HEAD_W]
    o = o_all[0:tq, :] - lam * o_all[tq:2 * tq, :]
    o_ref[0] = (_rms(o, gsub_ref[...]) * (1.0 - lam_init)).astype(o_ref.dtype)


def _attention(qkv, rel_bias, g_sub, lq1, lk1, lq2, lk2, lam_init):
    b, t, _ = qkv.shape
    tq = min(512, t // 4)
    tk = 2 * tq
    nk = t // tk
    assert tq >= 128
    small = pl.BlockSpec((1, QK_DIM), lambda bi, h, qi: (0, 0))
    kern = functools.partial(_attn_kernel, tq=tq, tk=tk, nk=nk, lam_init=lam_init)
    return pl.pallas_call(
        kern,
        grid=(b, HEADS, t // tq),
        in_specs=[
            pl.BlockSpec(memory_space=pltpu.SMEM),
            pl.BlockSpec((1, tq, HEAD_W), lambda bi, h, qi: (bi, qi, h)),
            pl.BlockSpec((1, t, HEAD_W), lambda bi, h, qi: (bi, 0, HEADS + h)),
            pl.BlockSpec((1, t, HEAD_W), lambda bi, h, qi: (bi, 0, 2 * HEADS + h)),
            pl.BlockSpec((1, HEAD_W), lambda bi, h, qi: (0, 0)),
            small, small, small, small,
        ],
        out_specs=pl.BlockSpec((1, tq, HEAD_W), lambda bi, h, qi: (bi, qi, h)),
        out_shape=jax.ShapeDtypeStruct((b, t, HEADS * HEAD_W), BF16),
        scratch_shapes=[
            pltpu.VMEM((2 * tq, HEAD_W), BF16),
            pltpu.VMEM((t, 2 * HEAD_W), BF16),
            pltpu.VMEM((tk // tq + 4, tq, tk), F32),
            pltpu.VMEM((2, 2 * tq, tk), F32),
            pltpu.VMEM((2, 2 * tq, LANES), F32),
            pltpu.VMEM((2 * tq, LANES), F32),
            pltpu.VMEM((2 * tq, LANES), F32),
            pltpu.VMEM((2 * tq, 2 * HEAD_W), F32),
        ],
        compiler_params=_cparams(("parallel", "parallel", "arbitrary")),
        name="diff_attention",
    )(rel_bias, qkv, qkv, qkv, g_sub, lq1, lk1, lq2, lk2)


def _shift_rows(x, k):
    return pltpu.roll(x, k % x.shape[0], 0)


def _block_bcast(x, pos, c, rev):
    src = c if rev else c - 1
    v = jnp.where(pos == src, x, 0.0)
    k = 1
    while k < c:
        v = v + _shift_rows(v, k if rev else -k)
        k *= 2
    return v + _shift_rows(v, -c if rev else c)


def _hgrn_chunk(q, kk, g, v, st, rev):
    shape = q.shape
    row = lax.broadcasted_iota(jnp.int32, shape, 0)
    ti = lax.broadcasted_iota(jnp.int32, (CHUNK, CHUNK), 0)
    si = lax.broadcasted_iota(jnp.int32, (CHUNK, CHUNK), 1)
    qb = q.astype(BF16)
    kb = kk.astype(BF16)
    a = jnp.where(ti == si, _dot_nt(qb, kb), 0.0)
    state = g
    c = 1
    while c < CHUNK:
        pos = row & (2 * c - 1)
        is_q = (pos < c) if rev else (pos >= c)
        tp = _block_bcast(state, pos, c, rev)
        e = jnp.exp(jnp.where(is_q, state, tp - state))
        qf = jnp.where(is_q, q * e, 0.0).astype(BF16)
        kf = jnp.where(is_q, 0.0, kk * e).astype(BF16)
        shift = int(math.log2(2 * c))
        same = (ti >> shift) == (si >> shift)
        a = a + jnp.where(same, _dot_nt(qf, kf), 0.0)
        state = state + jnp.where(is_q, tp, 0.0)
        c *= 2
    edge = 0 if rev else CHUNK - 1
    tot = state[edge:edge + 1, :]
    qe = (q * jnp.exp(state)).astype(BF16)
    kd = (kk * jnp.exp(tot - state)).astype(BF16)
    vb = v.astype(BF16)
    o = _dot_nt(qe, st.astype(BF16)) + jnp.dot(a.astype(BF16), vb, preferred_element_type=F32)
    st_new = st * jnp.exp(tot) + _dot_tn(vb, kd)
    return o, st_new


def _hgrn_chunk_midnorm(q, kk, cs, v, st, rev):
    half = CHUNK // 2
    mid = half if rev else half - 1
    edge = 0 if rev else CHUNK - 1
    ti = lax.broadcasted_iota(jnp.int32, (CHUNK, CHUNK), 0)
    si = lax.broadcasted_iota(jnp.int32, (CHUNK, CHUNK), 1)
    m = cs[mid:mid + 1, :]
    tot = cs[edge:edge + 1, :]
    qn = q * jnp.exp(cs - m)
    kn = kk * jnp.exp(m - cs)
    a = _dot_nt(qn.astype(BF16), kn.astype(BF16))
    a = jnp.where((si >= ti) if rev else (si <= ti), a, 0.0)
    qe = (qn * jnp.exp(m)).astype(BF16)
    kd = (kn * jnp.exp(tot - m)).astype(BF16)
    vb = v.astype(BF16)
    o = _dot_nt(qe, st.astype(BF16)) + jnp.dot(a.astype(BF16), vb, preferred_element_type=F32)
    st_new = st * jnp.exp(tot) + _dot_tn(vb, kd)
    return o, st_new


def _hgrn_scan(q_ref, z_ref, i_ref, lb_ref, st_ref, kk_ref, g_ref, cs_ref, emit, *, tb, rev):
    @pl.when(pl.program_id(2) == 0)
    def _():
        st_ref[...] = jnp.zeros(st_ref.shape, F32)

    lbx = lb_ref[...]
    ex = jnp.exp(lbx - jnp.max(lbx, axis=0, keepdims=True))
    lb = ex[0:1, :] / jnp.sum(ex, axis=0, keepdims=True)
    nchunk = tb // CHUNK

    f = lb + (1.0 - lb) * jax.nn.sigmoid(z_ref[0])
    g = jnp.log(f)
    kk_ref[...] = 1.0 - f
    g_ref[...] = g
    pos = lax.broadcasted_iota(jnp.int32, g.shape, 0) & (CHUNK - 1)
    cs = g
    k = 1
    while k < CHUNK:
        if rev:
            cs = cs + jnp.where(pos < CHUNK - k, _shift_rows(cs, -k), 0.0)
        else:
            cs = cs + jnp.where(pos >= k, _shift_rows(cs, k), 0.0)
        k *= 2
    cs_ref[...] = cs
    half = CHUNK // 2
    mid = half if rev else half - 1
    edge = 0 if rev else CHUNK - 1
    cs3 = cs.reshape(nchunk, CHUNK, cs.shape[-1])
    spread = jnp.max(jnp.maximum(-cs3[:, mid, :], cs3[:, mid, :] - cs3[:, edge, :]))
    safe = spread <= SAFE_RANGE

    def run(chunk_fn, aux_ref, unroll):
        def body(j, carry):
            jj = (nchunk - 1 - j) if rev else j
            rows = pl.ds(pl.multiple_of(jj * CHUNK, CHUNK), CHUNK)
            o, st_new = chunk_fn(q_ref[0, rows, :], kk_ref[rows, :], aux_ref[rows, :],
                                 i_ref[0, rows, :], st_ref[...], rev)
            st_ref[...] = st_new
            emit(rows, o)
            return carry

        lax.fori_loop(0, nchunk, body, 0, unroll=unroll)

    @pl.when(safe)
    def _():
        run(_hgrn_chunk_midnorm, cs_ref, min(16, nchunk))

    @pl.when(jnp.logical_not(safe))
    def _():
        run(_hgrn_chunk, g_ref, 1)


def _hgrn_fwd_kernel(q_ref, z_ref, i_ref, lb_ref, o_ref, st_ref, kk_ref, g_ref, cs_ref, *, tb):
    def emit(rows, o):
        o_ref[0, rows, :] = o

    _hgrn_scan(q_ref, z_ref, i_ref, lb_ref, st_ref, kk_ref, g_ref, cs_ref, emit, tb=tb, rev=False)


def _hgrn_bwd_kernel(q_ref, z_ref, i_ref, lb_ref, of_ref, og_ref, gout_ref, o_ref, st_ref, kk_ref,
                     g_ref, cs_ref, *, tb):
    def emit(rows, o):
        y = _rms(o + of_ref[0, rows, :], gout_ref[...])
        o_ref[0, rows, :] = (y * jax.nn.silu(og_ref[0, rows, :])).astype(o_ref.dtype)

    _hgrn_scan(q_ref, z_ref, i_ref, lb_ref, st_ref, kk_ref, g_ref, cs_ref, emit, tb=tb, rev=True)


def _hgrn2(proj, lb_fwd, lb_bwd, g_out):
    b, t, _ = proj.shape
    tb = min(2048, t)
    nb = t // tb

    def col(blk, rev):
        if rev:
            return pl.BlockSpec((1, tb, HEAD_W), lambda bi, h, ti: (bi, nb - 1 - ti, blk * HEADS + h))
        return pl.BlockSpec((1, tb, HEAD_W), lambda bi, h, ti: (bi, ti, blk * HEADS + h))

    lb_spec = pl.BlockSpec((lb_fwd.shape[0], HEAD_W), lambda bi, h, ti: (0, h))
    sem = _cparams(("parallel", "parallel", "arbitrary"))
    scratch = [pltpu.VMEM((HEAD_W, HEAD_W), F32)] + [pltpu.VMEM((tb, HEAD_W), F32)] * 3
    o_fwd = pl.pallas_call(
        functools.partial(_hgrn_fwd_kernel, tb=tb),
        grid=(b, HEADS, nb),
        in_specs=[col(0, False), col(1, False), col(3, False), lb_spec],
        out_specs=pl.BlockSpec((1, tb, HEAD_W), lambda bi, h, ti: (bi, ti, h)),
        out_shape=jax.ShapeDtypeStruct((b, t, HEADS * HEAD_W), F32),
        scratch_shapes=scratch,
        compiler_params=sem,
        name="hgrn_fwd",
    )(proj, proj, proj, lb_fwd)
    rev_out = pl.BlockSpec((1, tb, HEAD_W), lambda bi, h, ti: (bi, nb - 1 - ti, h))
    return pl.pallas_call(
        functools.partial(_hgrn_bwd_kernel, tb=tb),
        grid=(b, HEADS, nb),
        in_specs=[col(0, True), col(2, True), col(3, True), lb_spec, rev_out, col(4, True),
                  pl.BlockSpec((1, HEAD_W), lambda bi, h, ti: (0, 0))],
        out_specs=rev_out,
        out_shape=jax.ShapeDtypeStruct((b, t, HEADS * HEAD_W), BF16),
        scratch_shapes=scratch,
        compiler_params=sem,
        name="hgrn_bwd",
    )(proj, proj, proj, lb_bwd, o_fwd, proj, g_out)


def _merge_kernel(x_ref, oa_ref, ob_ref, ga_ref, gb_ref, wa_ref, wb_ref, wo_ref, g_ref, o_ref):
    pa = jnp.dot(oa_ref[...], wa_ref[...], preferred_element_type=F32)
    pb = jnp.dot(ob_ref[...], wb_ref[...], preferred_element_type=F32)
    merged = jax.nn.sigmoid(ga_ref[...]) * pa + jax.nn.sigmoid(gb_ref[...]) * pb
    t = jnp.dot(merged.astype(BF16), wo_ref[...], preferred_element_type=F32)
    o_ref[...] = x_ref[...] + _rms(t, g_ref[...])


def _merge(x2, oa, ob, proj2, wa, wb, wo, g):
    n, d = x2.shape
    tm = min(512, n)
    row = lambda blk: pl.BlockSpec((tm, d), lambda i: (i, blk))
    full = pl.BlockSpec((d, d), lambda i: (0, 0))
    return pl.pallas_call(
        _merge_kernel,
        grid=(n // tm,),
        in_specs=[row(0), row(0), row(0), row(5), row(6), full, full, full,
                  pl.BlockSpec((1, d), lambda i: (0, 0))],
        out_specs=row(0),
        out_shape=jax.ShapeDtypeStruct((n, d), F32),
        compiler_params=_cparams(("parallel",)),
        name="merge_out",
    )(x2, oa, ob, proj2, proj2, wa, wb, wo, g)


def _mlp_kernel(x_ref, gpre_ref, wu_ref, wd_ref, gpost_ref, o_ref, hn_ref, acc_ref):
    j = pl.program_id(1)

    @pl.when(j == 0)
    def _():
        hn_ref[...] = _rms(x_ref[...], gpre_ref[...]).astype(BF16)
        acc_ref[...] = jnp.zeros(acc_ref.shape, F32)

    u = jnp.square(jnp.maximum(jnp.dot(hn_ref[...], wu_ref[...], preferred_element_type=F32), 0.0))
    acc_ref[...] += jnp.dot(u.astype(BF16), wd_ref[...], preferred_element_type=F32)

    @pl.when(j == pl.num_programs(1) - 1)
    def _():
        o_ref[...] = x_ref[...] + _rms(acc_ref[...], gpost_ref[...])


def _mlp(x2, gpre, wu, wd, gpost):
    n, d = x2.shape
    tm = min(1024, n)
    tf = 1024
    vec = pl.BlockSpec((1, d), lambda i, j: (0, 0))
    return pl.pallas_call(
        _mlp_kernel,
        grid=(n // tm, D_FF // tf),
        in_specs=[pl.BlockSpec((tm, d), lambda i, j: (i, 0)), vec,
                  pl.BlockSpec((d, tf), lambda i, j: (0, j)),
                  pl.BlockSpec((tf, d), lambda i, j: (j, 0)), vec],
        out_specs=pl.BlockSpec((tm, d), lambda i, j: (i, 0)),
        out_shape=jax.ShapeDtypeStruct((n, d), F32),
        scratch_shapes=[pltpu.VMEM((tm, d), BF16), pltpu.VMEM((tm, d), F32)],
        compiler_params=_cparams(("parallel", "arbitrary")),
        name="mlp",
    )(x2, gpre, wu, wd, gpost)


def _encoder_layer(x, l, p):
    b, t, d = x.shape
    n = b * t
    x2 = x.reshape(n, d)
    n_attn = 3 * HEADS * HEAD_W
    w_in = p["w_in"][l]
    width = HEADS * HEAD_W
    attn_scale = jnp.concatenate([jnp.full((1, width), QK_DIM ** -0.5, F32),
                                  jnp.full((1, width), LOG2E, F32), jnp.ones((1, width), F32)], axis=1)
    qkv = _inproj(x2, p["g_mix_pre"][l:l + 1], w_in[:, :n_attn].astype(BF16), attn_scale, BF16)
    proj = _inproj(x2, p["g_mix_pre"][l:l + 1], w_in[:, n_attn:].astype(BF16),
                   jnp.ones((1, w_in.shape[1] - n_attn), F32), F32)
    lam_init = 0.8 - 0.6 * math.exp(-0.3 * l)
    o_a = _attention(qkv.reshape(b, t, n_attn), p["rel_bias"], p["g_attn_sub"][l:l + 1],
                     p["lam_q1"][l:l + 1], p["lam_k1"][l:l + 1], p["lam_q2"][l:l + 1],
                     p["lam_k2"][l:l + 1], lam_init)
    o_b = _hgrn2(proj.reshape(b, t, proj.shape[1]), p["lb_fwd"], p["lb_bwd"], p["g_hgrn_out"][l:l + 1])
    x1 = _merge(x2, o_a.reshape(n, d), o_b.reshape(n, d), proj,
                p["w_proj_a"][l].astype(BF16), p["w_proj_b"][l].astype(BF16),
                p["w_out"][l].astype(BF16), p["g_mix_post"][l:l + 1])
    y = _mlp(x1, p["g_mlp_pre"][l:l + 1], p["w_mlp_up"][l].astype(BF16),
             p["w_mlp_down"][l].astype(BF16), p["g_mlp_post"][l:l + 1])
    return y.reshape(b, t, d)


def kernel(x_prompt, x_sample, rel_bias, g_mix_pre, w_in, lam_q1, lam_k1, lam_q2, lam_k2, g_attn_sub, lb_fwd, lb_bwd, g_hgrn_out, w_proj_a, w_proj_b, w_out, g_mix_post, g_mlp_pre, w_mlp_up, w_mlp_down, g_mlp_post):
    p = dict(rel_bias=rel_bias, g_mix_pre=g_mix_pre, w_in=w_in, lam_q1=lam_q1, lam_k1=lam_k1,
             lam_q2=lam_q2, lam_k2=lam_k2, g_attn_sub=g_attn_sub, lb_fwd=lb_fwd, lb_bwd=lb_bwd,
             g_hgrn_out=g_hgrn_out, w_proj_a=w_proj_a, w_proj_b=w_proj_b, w_out=w_out,
             g_mix_post=g_mix_post, g_mlp_pre=g_mlp_pre, w_mlp_up=w_mlp_up, w_mlp_down=w_mlp_down,
             g_mlp_post=g_mlp_post)

    def trunk(x):
        for l in range(g_mix_pre.shape[0]):
            x = _encoder_layer(x, l, p)
        return x

    return (trunk(x_prompt), trunk(x_sample))
```

```python
import functools
import math

import jax
import jax.numpy as jnp
from jax import lax
from jax.experimental import pallas as pl
from jax.experimental.pallas import tpu as pltpu

D_MODEL = 1024
HEADS = 8
HEAD_W = 128
QK_DIM = 64
NUM_BUCKETS = 32
D_FF = 4 * D_MODEL
EPS = 1e-6
CHUNK = 64
LANES = 128
LOG2E = 1.4426950408889634
SAFE_RANGE = 60.0
VMEM_LIMIT = 56 * 1024 * 1024

F32 = jnp.float32
BF16 = jnp.bfloat16


def _cparams(sem):
    return pltpu.CompilerParams(dimension_semantics=sem, vmem_limit_bytes=VMEM_LIMIT)


def _rms(x, g):
    return x * lax.rsqrt(jnp.mean(x * x, axis=-1, keepdims=True) + EPS) * g


def _dot_nt(a, b):
    return lax.dot_general(a, b, (((1,), (1,)), ((), ())), preferred_element_type=F32)


def _dot_tn(a, b):
    return lax.dot_general(a, b, (((0,), (0,)), ((), ())), preferred_element_type=F32)


def _inproj_kernel(x_ref, g_ref, w_ref, cs_ref, o_ref, xn_ref):
    @pl.when(pl.program_id(1) == 0)
    def _():
        xn_ref[...] = _rms(x_ref[...], g_ref[...]).astype(BF16)

    acc = jnp.dot(xn_ref[...], w_ref[...], preferred_element_type=F32)
    o_ref[...] = (acc * cs_ref[...]).astype(o_ref.dtype)


def _inproj(x2, g, w, col_scale, out_dtype):
    n, d = x2.shape
    nout = w.shape[1]
    tm = min(1024, n)
    tn = max(c for c in range(256, 2049, 256) if nout % c == 0)
    return pl.pallas_call(
        _inproj_kernel,
        grid=(n // tm, nout // tn),
        in_specs=[
            pl.BlockSpec((tm, d), lambda i, j: (i, 0)),
            pl.BlockSpec((1, d), lambda i, j: (0, 0)),
            pl.BlockSpec((d, tn), lambda i, j: (0, j)),
            pl.BlockSpec((1, tn), lambda i, j: (0, j)),
        ],
        out_specs=pl.BlockSpec((tm, tn), lambda i, j: (i, j)),
        out_shape=jax.ShapeDtypeStruct((n, nout), out_dtype),
        scratch_shapes=[pltpu.VMEM((tm, d), BF16)],
        compiler_params=_cparams(("parallel", "arbitrary")),
        name="inproj",
    )(x2, g, w, col_scale)


def _t5_bias_row(d, relb_ref, h):
    n = jnp.abs(d)
    n2 = n * n
    big = jnp.full(d.shape, 8, jnp.int32)
    for m in range(1, 8):
        big = big + (n2 >= 64 * 2 ** m).astype(jnp.int32)
    big = jnp.minimum(big, NUM_BUCKETS // 2 - 1)
    bucket = jnp.where(d > 0, NUM_BUCKETS // 2, 0) + jnp.where(n < 8, n, big)
    val = jnp.zeros(d.shape, F32)
    for b in range(NUM_BUCKETS):
        val = jnp.where(bucket == b, relb_ref[b, h], val)
    return val


def _attn_kernel(relb_ref, q_ref, k_ref, v_ref, gsub_ref, lq1_ref, lk1_ref, lq2_ref, lk2_ref,
                 o_ref, q2_ref, vext_ref, bias_ref, s_ref, mcur_ref, alpha_ref, m_ref, acc_ref,
                 *, tq, tk, nk, lam_init):
    h = pl.program_id(1)
    qi = pl.program_id(2)
    t = nk * tk
    rb = LANES
    ntile = tk // LANES
    ratio = tk // tq
    n_band = ratio + 2

    @pl.when(qi == 0)
    def _():
        vext_ref[:, 0:HEAD_W] = v_ref[0]
        vext_ref[:, HEAD_W:2 * HEAD_W] = jnp.ones((t, HEAD_W), BF16)
        bias_ref[0] = jnp.full((tq, tk), relb_ref[NUM_BUCKETS // 2 - 1, h] * LOG2E, F32)
        bias_ref[n_band + 1] = jnp.full((tq, tk), relb_ref[NUM_BUCKETS - 1, h] * LOG2E, F32)
        width = tq + tk
        col = lax.broadcasted_iota(jnp.int32, (1, width), 1)
        for idx in range(n_band):
            delta = (idx - ratio) * tq
            row_bias = _t5_bias_row(delta + col - tq, relb_ref, h) * LOG2E
            tab = jnp.broadcast_to(row_bias, (tq, width))
            tab = pltpu.roll(tab, 0, 1, stride=1, stride_axis=0)
            bias_ref[idx + 1] = tab[:, tq:tq + tk]

    q = q_ref[0]
    lane = lax.broadcasted_iota(jnp.int32, q.shape, 1)
    zero = jnp.zeros_like(q)
    q2_ref[0:tq, :] = jnp.where(lane < QK_DIM, q, zero)
    q2_ref[tq:2 * tq, :] = jnp.where(lane >= QK_DIM, q, zero)
    m_ref[...] = jnp.full(m_ref.shape, -jnp.inf, F32)
    acc_ref[...] = jnp.zeros(acc_ref.shape, F32)

    def row_max(x):
        mx = x[:, 0:LANES]
        for u in range(1, ntile):
            mx = jnp.maximum(mx, x[:, u * LANES:(u + 1) * LANES])
        return jnp.broadcast_to(jnp.max(mx, axis=1, keepdims=True), mx.shape)

    def scores(j, slot):
        tile = jnp.clip(ratio * j - qi + ratio, -1, n_band) + 1
        bias = bias_ref[tile]
        r0 = pl.multiple_of(j * tk, tk)
        sv = _dot_nt(q2_ref[...], k_ref[0, pl.ds(r0, tk), :])
        sv = sv + jnp.concatenate([bias, bias], axis=0)
        s_ref[slot] = sv
        mcur_ref[slot] = row_max(sv)

    def consume(j, slot):
        blocks = []
        for b in range(2 * tq // rb):
            rows = slice(b * rb, (b + 1) * rb)
            m_prev = m_ref[rows, :]
            m_new = jnp.maximum(m_prev, mcur_ref[slot, rows, :])
            alpha_ref[rows, :] = jnp.exp2(m_prev - m_new)
            m_ref[rows, :] = m_new
            blocks.append(jnp.concatenate(
                [jnp.exp2(s_ref[slot, rows, u * LANES:(u + 1) * LANES] - m_new).astype(BF16)
                 for u in range(ntile)], axis=1))
        p = jnp.concatenate(blocks, axis=0)
        r0 = pl.multiple_of(j * tk, tk)
        pv = jnp.dot(p, vext_ref[pl.ds(r0, tk), :], preferred_element_type=F32)
        alpha = alpha_ref[...]
        acc_ref[...] = jnp.concatenate([alpha, alpha], axis=1) * acc_ref[...] + pv

    scores(0, 0)

    def pair(jj, carry):
        scores(2 * jj + 1, 1)
        consume(2 * jj, 0)
        scores(2 * jj + 2, 0)
        consume(2 * jj + 1, 1)
        return carry

    lax.fori_loop(0, nk // 2 - 1, pair, 0)
    scores(nk - 1, 1)
    consume(nk - 2, 0)
    consume(nk - 1, 1)

    lam = (jnp.exp(jnp.sum(lq1_ref[...] * lk1_ref[...], axis=1, keepdims=True))
           - jnp.exp(jnp.sum(lq2_ref[...] * lk2_ref[...], axis=1, keepdims=True))
           + lam_init)
    acc = acc_ref[...]
    o_all = acc[:, 0:HEAD_W] / acc[:, HEAD_W:2 * HEAD_W]
    o = o_all[0:tq, :] - lam * o_all[tq:2 * tq, :]
    o_ref[0] = (_rms(o, gsub_ref[...]) * (1.0 - lam_init)).astype(o_ref.dtype)


def _attention(qkv, rel_bias, g_sub, lq1, lk1, lq2, lk2, lam_init):
    b, t, _ = qkv.shape
    tq = min(512, t // 4)
    tk = 2 * tq
    nk = t // tk
    assert tq >= 128
    small = pl.BlockSpec((1, QK_DIM), lambda bi, h, qi: (0, 0))
    kern = functools.partial(_attn_kernel, tq=tq, tk=tk, nk=nk, lam_init=lam_init)
    return pl.pallas_call(
        kern,
        grid=(b, HEADS, t // tq),
        in_specs=[
            pl.BlockSpec(memory_space=pltpu.SMEM),
            pl.BlockSpec((1, tq, HEAD_W), lambda bi, h, qi: (bi, qi, h)),
            pl.BlockSpec((1, t, HEAD_W), lambda bi, h, qi: (bi, 0, HEADS + h)),
            pl.BlockSpec((1, t, HEAD_W), lambda bi, h, qi: (bi, 0, 2 * HEADS + h)),
            pl.BlockSpec((1, HEAD_W), lambda bi, h, qi: (0, 0)),
            small, small, small, small,
        ],
        out_specs=pl.BlockSpec((1, tq, HEAD_W), lambda bi, h, qi: (bi, qi, h)),
        out_shape=jax.ShapeDtypeStruct((b, t, HEADS * HEAD_W), BF16),
        scratch_shapes=[
            pltpu.VMEM((2 * tq, HEAD_W), BF16),
            pltpu.VMEM((t, 2 * HEAD_W), BF16),
            pltpu.VMEM((tk // tq + 4, tq, tk), F32),
            pltpu.VMEM((2, 2 * tq, tk), F32),
            pltpu.VMEM((2, 2 * tq, LANES), F32),
            pltpu.VMEM((2 * tq, LANES), F32),
            pltpu.VMEM((2 * tq, LANES), F32),
            pltpu.VMEM((2 * tq, 2 * HEAD_W), F32),
        ],
        compiler_params=_cparams(("parallel", "parallel", "arbitrary")),
        name="diff_attention",
    )(rel_bias, qkv, qkv, qkv, g_sub, lq1, lk1, lq2, lk2)


def _shift_rows(x, k):
    return pltpu.roll(x, k % x.shape[0], 0)


def _block_bcast(x, pos, c, rev):
    src = c if rev else c - 1
    v = jnp.where(pos == src, x, 0.0)
    k = 1
    while k < c:
        v = v + _shift_rows(v, k if rev else -k)
        k *= 2
    return v + _shift_rows(v, -c if rev else c)


def _hgrn_chunk(q, kk, g, v, st, rev):
    shape = q.shape
    row = lax.broadcasted_iota(jnp.int32, shape, 0)
    ti = lax.broadcasted_iota(jnp.int32, (CHUNK, CHUNK), 0)
    si = lax.broadcasted_iota(jnp.int32, (CHUNK, CHUNK), 1)
    qb = q.astype(BF16)
    kb = kk.astype(BF16)
    a = jnp.where(ti == si, _dot_nt(qb, kb), 0.0)
    state = g
    c = 1
    while c < CHUNK:
        pos = row & (2 * c - 1)
        is_q = (pos < c) if rev else (pos >= c)
        tp = _block_bcast(state, pos, c, rev)
        e = jnp.exp(jnp.where(is_q, state, tp - state))
        qf = jnp.where(is_q, q * e, 0.0).astype(BF16)
        kf = jnp.where(is_q, 0.0, kk * e).astype(BF16)
        shift = int(math.log2(2 * c))
        same = (ti >> shift) == (si >> shift)
        a = a + jnp.where(same, _dot_nt(qf, kf), 0.0)
        state = state + jnp.where(is_q, tp, 0.0)
        c *= 2
    edge = 0 if rev else CHUNK - 1
    tot = state[edge:edge + 1, :]
    qe = (q * jnp.exp(state)).astype(BF16)
    kd = (kk * jnp.exp(tot - state)).astype(BF16)
    vb = v.astype(BF16)
    o = _dot_nt(qe, st.astype(BF16)) + jnp.dot(a.astype(BF16), vb, preferred_element_type=F32)
    st_new = st * jnp.exp(tot) + _dot_tn(vb, kd)
    return o, st_new


def _hgrn_chunk_midnorm(q, kk, cs, v, st, rev):
    half = CHUNK // 2
    mid = half if rev else half - 1
    edge = 0 if rev else CHUNK - 1
    ti = lax.broadcasted_iota(jnp.int32, (CHUNK, CHUNK), 0)
    si = lax.broadcasted_iota(jnp.int32, (CHUNK, CHUNK), 1)
    m = cs[mid:mid + 1, :]
    tot = cs[edge:edge + 1, :]
    qn = q * jnp.exp(cs - m)
    kn = kk * jnp.exp(m - cs)
    a = _dot_nt(qn.astype(BF16), kn.astype(BF16))
    a = jnp.where((si >= ti) if rev else (si <= ti), a, 0.0)
    qe = (qn * jnp.exp(m)).astype(BF16)
    kd = (kn * jnp.exp(tot - m)).astype(BF16)
    vb = v.astype(BF16)
    o = _dot_nt(qe, st.astype(BF16)) + jnp.dot(a.astype(BF16), vb, preferred_element_type=F32)
    st_new = st * jnp.exp(tot) + _dot_tn(vb, kd)
    return o, st_new


def _hgrn_scan(q_ref, z_ref, i_ref, lb_ref, st_ref, kk_ref, g_ref, cs_ref, emit, *, tb, rev):
    @pl.when(pl.program_id(2) == 0)
    def _():
        st_ref[...] = jnp.zeros(st_ref.shape, F32)

    lbx = lb_ref[...]
    ex = jnp.exp(lbx - jnp.max(lbx, axis=0, keepdims=True))
    lb = ex[0:1, :] / jnp.sum(ex, axis=0, keepdims=True)
    nchunk = tb // CHUNK

    f = lb + (1.0 - lb) * jax.nn.sigmoid(z_ref[0])
    g = jnp.log(f)
    kk_ref[...] = 1.0 - f
    g_ref[...] = g
    pos = lax.broadcasted_iota(jnp.int32, g.shape, 0) & (CHUNK - 1)
    cs = g
    k = 1
    while k < CHUNK:
        if rev:
            cs = cs + jnp.where(pos < CHUNK - k, _shift_rows(cs, -k), 0.0)
        else:
            cs = cs + jnp.where(pos >= k, _shift_rows(cs, k), 0.0)
        k *= 2
    cs_ref[...] = cs
    half = CHUNK // 2
    mid = half if rev else half - 1
    edge = 0 if rev else CHUNK - 1
    cs3 = cs.reshape(nchunk, CHUNK, cs.shape[-1])
    spread = jnp.max(jnp.maximum(-cs3[:, mid, :], cs3[:, mid, :] - cs3[:, edge, :]))
    safe = spread <= SAFE_RANGE

    def run(chunk_fn, aux_ref, unroll):
        def body(j, carry):
            jj = (nchunk - 1 - j) if rev else j
            rows = pl.ds(pl.multiple_of(jj * CHUNK, CHUNK), CHUNK)
            o, st_new = chunk_fn(q_ref[0, rows, :], kk_ref[rows, :], aux_ref[rows, :],
                                 i_ref[0, rows, :], st_ref[...], rev)
            st_ref[...] = st_new
            emit(rows, o)
            return carry

        lax.fori_loop(0, nchunk, body, 0, unroll=unroll)

    @pl.when(safe)
    def _():
        run(_hgrn_chunk_midnorm, cs_ref, min(16, nchunk))

    @pl.when(jnp.logical_not(safe))
    def _():
        run(_hgrn_chunk, g_ref, 1)


def _hgrn_fwd_kernel(q_ref, z_ref, i_ref, lb_ref, o_ref, st_ref, kk_ref, g_ref, cs_ref, *, tb):
    def emit(rows, o):
        o_ref[0, rows, :] = o

    _hgrn_scan(q_ref, z_ref, i_ref, lb_ref, st_ref, kk_ref, g_ref, cs_ref, emit, tb=tb, rev=False)


def _hgrn_bwd_kernel(q_ref, z_ref, i_ref, lb_ref, of_ref, og_ref, gout_ref, o_ref, st_ref, kk_ref,
                     g_ref, cs_ref, *, tb):
    def emit(rows, o):
        y = _rms(o + of_ref[0, rows, :], gout_ref[...])
        o_ref[0, rows, :] = (y * jax.nn.silu(og_ref[0, rows, :])).astype(o_ref.dtype)

    _hgrn_scan(q_ref, z_ref, i_ref, lb_ref, st_ref, kk_ref, g_ref, cs_ref, emit, tb=tb, rev=True)


def _hgrn2(proj, lb_fwd, lb_bwd, g_out):
    b, t, _ = proj.shape
    tb = min(2048, t)
    nb = t // tb

    def col(blk, rev):
        if rev:
            return pl.BlockSpec((1, tb, HEAD_W), lambda bi, h, ti: (bi, nb - 1 - ti, blk * HEADS + h))
        return pl.BlockSpec((1, tb, HEAD_W), lambda bi, h, ti: (bi, ti, blk * HEADS + h))

    lb_spec = pl.BlockSpec((lb_fwd.shape[0], HEAD_W), lambda bi, h, ti: (0, h))
    sem = _cparams(("parallel", "parallel", "arbitrary"))
    scratch = [pltpu.VMEM((HEAD_W, HEAD_W), F32)] + [pltpu.VMEM((tb, HEAD_W), F32)] * 3
    o_fwd = pl.pallas_call(
        functools.partial(_hgrn_fwd_kernel, tb=tb),
        grid=(b, HEADS, nb),
        in_specs=[col(0, False), col(1, False), col(3, False), lb_spec],
        out_specs=pl.BlockSpec((1, tb, HEAD_W), lambda bi, h, ti: (bi, ti, h)),
        out_shape=jax.ShapeDtypeStruct((b, t, HEADS * HEAD_W), F32),
        scratch_shapes=scratch,
        compiler_params=sem,
        name="hgrn_fwd",
    )(proj, proj, proj, lb_fwd)
    rev_out = pl.BlockSpec((1, tb, HEAD_W), lambda bi, h, ti: (bi, nb - 1 - ti, h))
    return pl.pallas_call(
        functools.partial(_hgrn_bwd_kernel, tb=tb),
        grid=(b, HEADS, nb),
        in_specs=[col(0, True), col(2, True), col(3, True), lb_spec, rev_out, col(4, True),
                  pl.BlockSpec((1, HEAD_W), lambda bi, h, ti: (0, 0))],
        out_specs=rev_out,
        out_shape=jax.ShapeDtypeStruct((b, t, HEADS * HEAD_W), BF16),
        scratch_shapes=scratch,
        compiler_params=sem,
        name="hgrn_bwd",
    )(proj, proj, proj, lb_bwd, o_fwd, proj, g_out)


def _merge_kernel(x_ref, oa_ref, ob_ref, ga_ref, gb_ref, wa_ref, wb_ref, wo_ref, g_ref, o_ref):
    pa = jnp.dot(oa_ref[...], wa_ref[...], preferred_element_type=F32)
    pb = jnp.dot(ob_ref[...], wb_ref[...], preferred_element_type=F32)
    merged = jax.nn.sigmoid(ga_ref[...]) * pa + jax.nn.sigmoid(gb_ref[...]) * pb
    t = jnp.dot(merged.astype(BF16), wo_ref[...], preferred_element_type=F32)
    o_ref[...] = x_ref[...] + _rms(t, g_ref[...])


def _merge(x2, oa, ob, proj2, wa, wb, wo, g):
    n, d = x2.shape
    tm = min(512, n)
    row = lambda blk: pl.BlockSpec((tm, d), lambda i: (i, blk))
    full = pl.BlockSpec((d, d), lambda i: (0, 0))
    return pl.pallas_call(
        _merge_kernel,
        grid=(n // tm,),
        in_specs=[row(0), row(0), row(0), row(5), row(6), full, full, full,
                  pl.BlockSpec((1, d), lambda i: (0, 0))],
        out_specs=row(0),
        out_shape=jax.ShapeDtypeStruct((n, d), F32),
        compiler_params=_cparams(("parallel",)),
        name="merge_out",
    )(x2, oa, ob, proj2, proj2, wa, wb, wo, g)


def _mlp_kernel(x_ref, gpre_ref, wu_ref, wd_ref, gpost_ref, o_ref, hn_ref, acc_ref):
    j = pl.program_id(1)

    @pl.when(j == 0)
    def _():
        hn_ref[...] = _rms(x_ref[...], gpre_ref[...]).astype(BF16)
        acc_ref[...] = jnp.zeros(acc_ref.shape, F32)

    u = jnp.square(jnp.maximum(jnp.dot(hn_ref[...], wu_ref[...], preferred_element_type=F32), 0.0))
    acc_ref[...] += jnp.dot(u.astype(BF16), wd_ref[...], preferred_element_type=F32)

    @pl.when(j == pl.num_programs(1) - 1)
    def _():
        o_ref[...] = x_ref[...] + _rms(acc_ref[...], gpost_ref[...])


def _mlp(x2, gpre, wu, wd, gpost):
    n, d = x2.shape
    tm = min(1024, n)
    tf = 1024
    vec = pl.BlockSpec((1, d), lambda i, j: (0, 0))
    return pl.pallas_call(
        _mlp_kernel,
        grid=(n // tm, D_FF // tf),
        in_specs=[pl.BlockSpec((tm, d), lambda i, j: (i, 0)), vec,
                  pl.BlockSpec((d, tf), lambda i, j: (0, j)),
                  pl.BlockSpec((tf, d), lambda i, j: (j, 0)), vec],
        out_specs=pl.BlockSpec((tm, d), lambda i, j: (i, 0)),
        out_shape=jax.ShapeDtypeStruct((n, d), F32),
        scratch_shapes=[pltpu.VMEM((tm, d), BF16), pltpu.VMEM((tm, d), F32)],
        compiler_params=_cparams(("parallel", "arbitrary")),
        name="mlp",
    )(x2, gpre, wu, wd, gpost)


def _encoder_layer(x, l, p):
    b, t, d = x.shape
    n = b * t
    x2 = x.reshape(n, d)
    n_attn = 3 * HEADS * HEAD_W
    w_in = p["w_in"][l]
    width = HEADS * HEAD_W
    attn_scale = jnp.concatenate([jnp.full((1, width), QK_DIM ** -0.5, F32),
                                  jnp.full((1, width), LOG2E, F32), jnp.ones((1, width), F32)], axis=1)
    qkv = _inproj(x2, p["g_mix_pre"][l:l + 1], w_in[:, :n_attn].astype(BF16), attn_scale, BF16)
    proj = _inproj(x2, p["g_mix_pre"][l:l + 1], w_in[:, n_attn:].astype(BF16),
                   jnp.ones((1, w_in.shape[1] - n_attn), F32), F32)
    lam_init = 0.8 - 0.6 * math.exp(-0.3 * l)
    o_a = _attention(qkv.reshape(b, t, n_attn), p["rel_bias"], p["g_attn_sub"][l:l + 1],
                     p["lam_q1"][l:l + 1], p["lam_k1"][l:l + 1], p["lam_q2"][l:l + 1],
                     p["lam_k2"][l:l + 1], lam_init)
    o_b = _hgrn2(proj.reshape(b, t, proj.shape[1]), p["lb_fwd"], p["lb_bwd"], p["g_hgrn_out"][l:l + 1])
    x1 = _merge(x2, o_a.reshape(n, d), o_b.reshape(n, d), proj,
                p["w_proj_a"][l].astype(BF16), p["w_proj_b"][l].astype(BF16),
                p["w_out"][l].astype(BF16), p["g_mix_post"][l:l + 1])
    y = _mlp(x1, p["g_mlp_pre"][l:l + 1], p["w_mlp_up"][l].astype(BF16),
             p["w_mlp_down"][l].astype(BF16), p["g_mlp_post"][l:l + 1])
    return y.reshape(b, t, d)


def kernel(x_prompt, x_sample, rel_bias, g_mix_pre, w_in, lam_q1, lam_k1, lam_q2, lam_k2, g_attn_sub, lb_fwd, lb_bwd, g_hgrn_out, w_proj_a, w_proj_b, w_out, g_mix_post, g_mlp_pre, w_mlp_up, w_mlp_down, g_mlp_post):
    p = dict(rel_bias=rel_bias, g_mix_pre=g_mix_pre, w_in=w_in, lam_q1=lam_q1, lam_k1=lam_k1,
             lam_q2=lam_q2, lam_k2=lam_k2, g_attn_sub=g_attn_sub, lb_fwd=lb_fwd, lb_bwd=lb_bwd,
             g_hgrn_out=g_hgrn_out, w_proj_a=w_proj_a, w_proj_b=w_proj_b, w_out=w_out,
             g_mix_post=g_mix_post, g_mlp_pre=g_mlp_pre, w_mlp_up=w_mlp_up, w_mlp_down=w_mlp_down,
             g_mlp_post=g_mlp_post)

    def trunk(x):
        for l in range(g_mix_pre.shape[0]):
            x = _encoder_layer(x, l, p)
        return x

    return (trunk(x_prompt), trunk(x_sample))
```

```python
import functools
import math

import jax
import jax.numpy as jnp
from jax import lax
from jax.experimental import pallas as pl
from jax.experimental.pallas import tpu as pltpu

D_MODEL = 1024
HEADS = 8
HEAD_W = 128
QK_DIM = 64
NUM_BUCKETS = 32
D_FF = 4 * D_MODEL
EPS = 1e-6
CHUNK = 64
LANES = 128
LOG2E = 1.4426950408889634
HGRN_HEADS_PER_STEP = 2
SAFE_RANGE = 60.0
VMEM_LIMIT = 56 * 1024 * 1024

F32 = jnp.float32
BF16 = jnp.bfloat16


def _cparams(sem):
    return pltpu.CompilerParams(dimension_semantics=sem, vmem_limit_bytes=VMEM_LIMIT)


def _rms(x, g):
    return x * lax.rsqrt(jnp.mean(x * x, axis=-1, keepdims=True) + EPS) * g


def _dot_nt(a, b):
    return lax.dot_general(a, b, (((1,), (1,)), ((), ())), preferred_element_type=F32)


def _dot_tn(a, b):
    return lax.dot_general(a, b, (((0,), (0,)), ((), ())), preferred_element_type=F32)


def _inproj_kernel(x_ref, g_ref, w_ref, cs_ref, o_ref, xn_ref):
    @pl.when(pl.program_id(1) == 0)
    def _():
        xn_ref[...] = _rms(x_ref[...], g_ref[...]).astype(BF16)

    acc = jnp.dot(xn_ref[...], w_ref[...], preferred_element_type=F32)
    o_ref[...] = (acc * cs_ref[...]).astype(o_ref.dtype)


def _inproj(x2, g, w, col_scale, out_dtype):
    n, d = x2.shape
    nout = w.shape[1]
    tm = min(1024, n)
    tn = max(c for c in range(256, 2049, 256) if nout % c == 0)
    return pl.pallas_call(
        _inproj_kernel,
        grid=(n // tm, nout // tn),
        in_specs=[
            pl.BlockSpec((tm, d), lambda i, j: (i, 0)),
            pl.BlockSpec((1, d), lambda i, j: (0, 0)),
            pl.BlockSpec((d, tn), lambda i, j: (0, j)),
            pl.BlockSpec((1, tn), lambda i, j: (0, j)),
        ],
        out_specs=pl.BlockSpec((tm, tn), lambda i, j: (i, j)),
        out_shape=jax.ShapeDtypeStruct((n, nout), out_dtype),
        scratch_shapes=[pltpu.VMEM((tm, d), BF16)],
        compiler_params=_cparams(("parallel", "arbitrary")),
        name="inproj",
    )(x2, g, w, col_scale)


def _t5_bias_row(d, relb_ref, h):
    n = jnp.abs(d)
    n2 = n * n
    big = jnp.full(d.shape, 8, jnp.int32)
    for m in range(1, 8):
        big = big + (n2 >= 64 * 2 ** m).astype(jnp.int32)
    big = jnp.minimum(big, NUM_BUCKETS // 2 - 1)
    bucket = jnp.where(d > 0, NUM_BUCKETS // 2, 0) + jnp.where(n < 8, n, big)
    val = jnp.zeros(d.shape, F32)
    for b in range(NUM_BUCKETS):
        val = jnp.where(bucket == b, relb_ref[b, h], val)
    return val


def _attn_kernel(relb_ref, q_ref, k_ref, v_ref, gsub_ref, lq1_ref, lk1_ref, lq2_ref, lk2_ref,
                 o_ref, q2_ref, vext_ref, bias_ref, s_ref, mcur_ref, alpha_ref, m_ref, acc_ref,
                 *, tq, tk, nk, lam_init):
    h = pl.program_id(1)
    qi = pl.program_id(2)
    t = nk * tk
    rb = LANES
    ntile = tk // LANES
    ratio = tk // tq
    n_band = ratio + 2

    @pl.when(qi == 0)
    def _():
        vext_ref[:, 0:HEAD_W] = v_ref[0]
        vext_ref[:, HEAD_W:2 * HEAD_W] = jnp.ones((t, HEAD_W), BF16)
        bias_ref[0] = jnp.full((tq, tk), relb_ref[NUM_BUCKETS // 2 - 1, h] * LOG2E, F32)
        bias_ref[n_band + 1] = jnp.full((tq, tk), relb_ref[NUM_BUCKETS - 1, h] * LOG2E, F32)
        width = tq + tk
        col = lax.broadcasted_iota(jnp.int32, (1, width), 1)
        for idx in range(n_band):
            delta = (idx - ratio) * tq
            row_bias = _t5_bias_row(delta + col - tq, relb_ref, h) * LOG2E
            tab = jnp.broadcast_to(row_bias, (tq, width))
            tab = pltpu.roll(tab, 0, 1, stride=1, stride_axis=0)
            bias_ref[idx + 1] = tab[:, tq:tq + tk]

    q = q_ref[0]
    lane = lax.broadcasted_iota(jnp.int32, q.shape, 1)
    zero = jnp.zeros_like(q)
    q2_ref[0:tq, :] = jnp.where(lane < QK_DIM, q, zero)
    q2_ref[tq:2 * tq, :] = jnp.where(lane >= QK_DIM, q, zero)
    m_ref[...] = jnp.full(m_ref.shape, -jnp.inf, F32)
    acc_ref[...] = jnp.zeros(acc_ref.shape, F32)

    def row_max(x):
        mx = x[:, 0:LANES]
        for u in range(1, ntile):
            mx = jnp.maximum(mx, x[:, u * LANES:(u + 1) * LANES])
        return jnp.broadcast_to(jnp.max(mx, axis=1, keepdims=True), mx.shape)

    def scores(j, slot):
        tile = jnp.clip(ratio * j - qi + ratio, -1, n_band) + 1
        bias = bias_ref[tile]
        r0 = pl.multiple_of(j * tk, tk)
        sv = _dot_nt(q2_ref[...], k_ref[0, pl.ds(r0, tk), :])
        sv = sv + jnp.concatenate([bias, bias], axis=0)
        s_ref[slot] = sv
        mcur_ref[slot] = row_max(sv)

    def consume(j, slot):
        blocks = []
        for b in range(2 * tq // rb):
            rows = slice(b * rb, (b + 1) * rb)
            m_prev = m_ref[rows, :]
            m_new = jnp.maximum(m_prev, mcur_ref[slot, rows, :])
            alpha_ref[rows, :] = jnp.exp2(m_prev - m_new)
            m_ref[rows, :] = m_new
            blocks.append(jnp.concatenate(
                [jnp.exp2(s_ref[slot, rows, u * LANES:(u + 1) * LANES] - m_new).astype(BF16)
                 for u in range(ntile)], axis=1))
        p = jnp.concatenate(blocks, axis=0)
        r0 = pl.multiple_of(j * tk, tk)
        pv = jnp.dot(p, vext_ref[pl.ds(r0, tk), :], preferred_element_type=F32)
        alpha = alpha_ref[...]
        acc_ref[...] = jnp.concatenate([alpha, alpha], axis=1) * acc_ref[...] + pv

    scores(0, 0)

    def pair(jj, carry):
        scores(2 * jj + 1, 1)
        consume(2 * jj, 0)
        scores(2 * jj + 2, 0)
        consume(2 * jj + 1, 1)
        return carry

    lax.fori_loop(0, nk // 2 - 1, pair, 0)
    scores(nk - 1, 1)
    consume(nk - 2, 0)
    consume(nk - 1, 1)

    lam = (jnp.exp(jnp.sum(lq1_ref[...] * lk1_ref[...], axis=1, keepdims=True))
           - jnp.exp(jnp.sum(lq2_ref[...] * lk2_ref[...], axis=1, keepdims=True))
           + lam_init)
    acc = acc_ref[...]
    o_all = acc[:, 0:HEAD_W] / acc[:, HEAD_W:2 * HEAD_W]
    o = o_all[0:tq, :] - lam * o_all[tq:2 * tq, :]
    o_ref[0] = (_rms(o, gsub_ref[...]) * (1.0 - lam_init)).astype(o_ref.dtype)


def _attention(qkv, rel_bias, g_sub, lq1, lk1, lq2, lk2, lam_init):
    b, t, _ = qkv.shape
    tq = min(512, t // 4)
    tk = 2 * tq
    nk = t // tk
    assert tq >= 128
    small = pl.BlockSpec((1, QK_DIM), lambda bi, h, qi: (0, 0))
    kern = functools.partial(_attn_kernel, tq=tq, tk=tk, nk=nk, lam_init=lam_init)
    return pl.pallas_call(
        kern,
        grid=(b, HEADS, t // tq),
        in_specs=[
            pl.BlockSpec(memory_space=pltpu.SMEM),
            pl.BlockSpec((1, tq, HEAD_W), lambda bi, h, qi: (bi, qi, h)),
            pl.BlockSpec((1, t, HEAD_W), lambda bi, h, qi: (bi, 0, HEADS + h)),
            pl.BlockSpec((1, t, HEAD_W), lambda bi, h, qi: (bi, 0, 2 * HEADS + h)),
            pl.BlockSpec((1, HEAD_W), lambda bi, h, qi: (0, 0)),
            small, small, small, small,
        ],
        out_specs=pl.BlockSpec((1, tq, HEAD_W), lambda bi, h, qi: (bi, qi, h)),
        out_shape=jax.ShapeDtypeStruct((b, t, HEADS * HEAD_W), BF16),
        scratch_shapes=[
            pltpu.VMEM((2 * tq, HEAD_W), BF16),
            pltpu.VMEM((t, 2 * HEAD_W), BF16),
            pltpu.VMEM((tk // tq + 4, tq, tk), F32),
            pltpu.VMEM((2, 2 * tq, tk), F32),
            pltpu.VMEM((2, 2 * tq, LANES), F32),
            pltpu.VMEM((2 * tq, LANES), F32),
            pltpu.VMEM((2 * tq, LANES), F32),
            pltpu.VMEM((2 * tq, 2 * HEAD_W), F32),
        ],
        compiler_params=_cparams(("parallel", "parallel", "arbitrary")),
        name="diff_attention",
    )(rel_bias, qkv, qkv, qkv, g_sub, lq1, lk1, lq2, lk2)


def _shift_rows(x, k):
    return pltpu.roll(x, k % x.shape[0], 0)


def _block_bcast(x, pos, c, rev):
    src = c if rev else c - 1
    v = jnp.where(pos == src, x, 0.0)
    k = 1
    while k < c:
        v = v + _shift_rows(v, k if rev else -k)
        k *= 2
    return v + _shift_rows(v, -c if rev else c)


def _hgrn_chunk(q, kk, g, v, st, rev):
    shape = q.shape
    row = lax.broadcasted_iota(jnp.int32, shape, 0)
    ti = lax.broadcasted_iota(jnp.int32, (CHUNK, CHUNK), 0)
    si = lax.broadcasted_iota(jnp.int32, (CHUNK, CHUNK), 1)
    qb = q.astype(BF16)
    kb = kk.astype(BF16)
    a = jnp.where(ti == si, _dot_nt(qb, kb), 0.0)
    state = g
    c = 1
    while c < CHUNK:
        pos = row & (2 * c - 1)
        is_q = (pos < c) if rev else (pos >= c)
        tp = _block_bcast(state, pos, c, rev)
        e = jnp.exp(jnp.where(is_q, state, tp - state))
        qf = jnp.where(is_q, q * e, 0.0).astype(BF16)
        kf = jnp.where(is_q, 0.0, kk * e).astype(BF16)
        shift = int(math.log2(2 * c))
        same = (ti >> shift) == (si >> shift)
        a = a + jnp.where(same, _dot_nt(qf, kf), 0.0)
        state = state + jnp.where(is_q, tp, 0.0)
        c *= 2
    edge = 0 if rev else CHUNK - 1
    tot = state[edge:edge + 1, :]
    qe = (q * jnp.exp(state)).astype(BF16)
    kd = (kk * jnp.exp(tot - state)).astype(BF16)
    vb = v.astype(BF16)
    o = _dot_nt(qe, st.astype(BF16)) + jnp.dot(a.astype(BF16), vb, preferred_element_type=F32)
    st_new = st * jnp.exp(tot) + _dot_tn(vb, kd)
    return o, st_new


def _hgrn_chunk_midnorm(q, kk, cs, v, st, rev):
    half = CHUNK // 2
    mid = half if rev else half - 1
    edge = 0 if rev else CHUNK - 1
    ti = lax.broadcasted_iota(jnp.int32, (CHUNK, CHUNK), 0)
    si = lax.broadcasted_iota(jnp.int32, (CHUNK, CHUNK), 1)
    m = cs[mid:mid + 1, :]
    tot = cs[edge:edge + 1, :]
    qn = q * jnp.exp(cs - m)
    kn = kk * jnp.exp(m - cs)
    a = _dot_nt(qn.astype(BF16), kn.astype(BF16))
    a = jnp.where((si >= ti) if rev else (si <= ti), a, 0.0)
    qe = (qn * jnp.exp(m)).astype(BF16)
    kd = (kn * jnp.exp(tot - m)).astype(BF16)
    vb = v.astype(BF16)
    o = _dot_nt(qe, st.astype(BF16)) + jnp.dot(a.astype(BF16), vb, preferred_element_type=F32)
    st_new = st * jnp.exp(tot) + _dot_tn(vb, kd)
    return o, st_new


def _hgrn_scan(q_ref, z_ref, i_ref, lb_ref, st_ref, kk_ref, g_ref, cs_ref, emit, *, tb, rev):
    @pl.when(pl.program_id(2) == 0)
    def _():
        st_ref[...] = jnp.zeros(st_ref.shape, F32)

    lbx = lb_ref[...]
    ex = jnp.exp(lbx - jnp.max(lbx, axis=0, keepdims=True))
    lb = ex[0:1, :] / jnp.sum(ex, axis=0, keepdims=True)
    nchunk = tb // CHUNK

    f = lb + (1.0 - lb) * jax.nn.sigmoid(z_ref[0])
    g = jnp.log(f)
    kk_ref[...] = 1.0 - f
    g_ref[...] = g
    pos = lax.broadcasted_iota(jnp.int32, g.shape, 0) & (CHUNK - 1)
    cs = g
    k = 1
    while k < CHUNK:
        if rev:
            cs = cs + jnp.where(pos < CHUNK - k, _shift_rows(cs, -k), 0.0)
        else:
            cs = cs + jnp.where(pos >= k, _shift_rows(cs, k), 0.0)
        k *= 2
    cs_ref[...] = cs
    half = CHUNK // 2
    mid = half if rev else half - 1
    edge = 0 if rev else CHUNK - 1
    cs3 = cs.reshape(nchunk, CHUNK, cs.shape[-1])
    spread = jnp.max(jnp.maximum(-cs3[:, mid, :], cs3[:, mid, :] - cs3[:, edge, :]))
    safe = spread <= SAFE_RANGE

    def run(chunk_fn, aux_ref, unroll):
        def body(j, carry):
            jj = (nchunk - 1 - j) if rev else j
            rows = pl.ds(pl.multiple_of(jj * CHUNK, CHUNK), CHUNK)
            for hh in range(HGRN_HEADS_PER_STEP):
                lanes = slice(hh * HEAD_W, (hh + 1) * HEAD_W)
                o, st_new = chunk_fn(q_ref[0, rows, lanes], kk_ref[rows, lanes], aux_ref[rows, lanes],
                                     i_ref[0, rows, lanes], st_ref[hh], rev)
                st_ref[hh] = st_new
                emit(rows, lanes, o)
            return carry

        lax.fori_loop(0, nchunk, body, 0, unroll=unroll)

    @pl.when(safe)
    def _():
        run(_hgrn_chunk_midnorm, cs_ref, min(8, nchunk))

    @pl.when(jnp.logical_not(safe))
    def _():
        run(_hgrn_chunk, g_ref, 1)


def _hgrn_fwd_kernel(q_ref, z_ref, i_ref, lb_ref, o_ref, st_ref, kk_ref, g_ref, cs_ref, *, tb):
    def emit(rows, lanes, o):
        o_ref[0, rows, lanes] = o

    _hgrn_scan(q_ref, z_ref, i_ref, lb_ref, st_ref, kk_ref, g_ref, cs_ref, emit, tb=tb, rev=False)


def _hgrn_bwd_kernel(q_ref, z_ref, i_ref, lb_ref, of_ref, og_ref, gout_ref, o_ref, st_ref, kk_ref,
                     g_ref, cs_ref, *, tb):
    def emit(rows, lanes, o):
        y = _rms(o + of_ref[0, rows, lanes], gout_ref[...])
        o_ref[0, rows, lanes] = (y * jax.nn.silu(og_ref[0, rows, lanes])).astype(o_ref.dtype)

    _hgrn_scan(q_ref, z_ref, i_ref, lb_ref, st_ref, kk_ref, g_ref, cs_ref, emit, tb=tb, rev=True)


def _hgrn2(proj, lb_fwd, lb_bwd, g_out):
    b, t, _ = proj.shape
    tb = min(2048, t)
    nb = t // tb

    hp = HGRN_HEADS_PER_STEP
    groups = HEADS // hp
    wd = hp * HEAD_W

    def col(blk, rev):
        if rev:
            return pl.BlockSpec((1, tb, wd), lambda bi, h, ti: (bi, nb - 1 - ti, blk * groups + h))
        return pl.BlockSpec((1, tb, wd), lambda bi, h, ti: (bi, ti, blk * groups + h))

    lb_spec = pl.BlockSpec((lb_fwd.shape[0], wd), lambda bi, h, ti: (0, h))
    sem = _cparams(("parallel", "parallel", "arbitrary"))
    scratch = [pltpu.VMEM((hp, HEAD_W, HEAD_W), F32)] + [pltpu.VMEM((tb, wd), F32)] * 3
    o_fwd = pl.pallas_call(
        functools.partial(_hgrn_fwd_kernel, tb=tb),
        grid=(b, groups, nb),
        in_specs=[col(0, False), col(1, False), col(3, False), lb_spec],
        out_specs=pl.BlockSpec((1, tb, wd), lambda bi, h, ti: (bi, ti, h)),
        out_shape=jax.ShapeDtypeStruct((b, t, HEADS * HEAD_W), F32),
        scratch_shapes=scratch,
        compiler_params=sem,
        name="hgrn_fwd",
    )(proj, proj, proj, lb_fwd)
    rev_out = pl.BlockSpec((1, tb, wd), lambda bi, h, ti: (bi, nb - 1 - ti, h))
    return pl.pallas_call(
        functools.partial(_hgrn_bwd_kernel, tb=tb),
        grid=(b, groups, nb),
        in_specs=[col(0, True), col(2, True), col(3, True), lb_spec, rev_out, col(4, True),
                  pl.BlockSpec((1, HEAD_W), lambda bi, h, ti: (0, 0))],
        out_specs=rev_out,
        out_shape=jax.ShapeDtypeStruct((b, t, HEADS * HEAD_W), BF16),
        scratch_shapes=scratch,
        compiler_params=sem,
        name="hgrn_bwd",
    )(proj, proj, proj, lb_bwd, o_fwd, proj, g_out)


def _merge_kernel(x_ref, oa_ref, ob_ref, ga_ref, gb_ref, wa_ref, wb_ref, wo_ref, g_ref, o_ref):
    pa = jnp.dot(oa_ref[...], wa_ref[...], preferred_element_type=F32)
    pb = jnp.dot(ob_ref[...], wb_ref[...], preferred_element_type=F32)
    merged = jax.nn.sigmoid(ga_ref[...]) * pa + jax.nn.sigmoid(gb_ref[...]) * pb
    t = jnp.dot(merged.astype(BF16), wo_ref[...], preferred_element_type=F32)
    o_ref[...] = x_ref[...] + _rms(t, g_ref[...])


def _merge(x2, oa, ob, proj2, wa, wb, wo, g):
    n, d = x2.shape
    tm = min(512, n)
    row = lambda blk: pl.BlockSpec((tm, d), lambda i: (i, blk))
    full = pl.BlockSpec((d, d), lambda i: (0, 0))
    return pl.pallas_call(
        _merge_kernel,
        grid=(n // tm,),
        in_specs=[row(0), row(0), row(0), row(5), row(6), full, full, full,
                  pl.BlockSpec((1, d), lambda i: (0, 0))],
        out_specs=row(0),
        out_shape=jax.ShapeDtypeStruct((n, d), F32),
        compiler_params=_cparams(("parallel",)),
        name="merge_out",
    )(x2, oa, ob, proj2, proj2, wa, wb, wo, g)


def _mlp_kernel(x_ref, gpre_ref, wu_ref, wd_ref, gpost_ref, o_ref, hn_ref, acc_ref):
    j = pl.program_id(1)

    @pl.when(j == 0)
    def _():
        hn_ref[...] = _rms(x_ref[...], gpre_ref[...]).astype(BF16)
        acc_ref[...] = jnp.zeros(acc_ref.shape, F32)

    u = jnp.square(jnp.maximum(jnp.dot(hn_ref[...], wu_ref[...], preferred_element_type=F32), 0.0))
    acc_ref[...] += jnp.dot(u.astype(BF16), wd_ref[...], preferred_element_type=F32)

    @pl.when(j == pl.num_programs(1) - 1)
    def _():
        o_ref[...] = x_ref[...] + _rms(acc_ref[...], gpost_ref[...])


def _mlp(x2, gpre, wu, wd, gpost):
    n, d = x2.shape
    tm = min(1024, n)
    tf = 1024
    vec = pl.BlockSpec((1, d), lambda i, j: (0, 0))
    return pl.pallas_call(
        _mlp_kernel,
        grid=(n // tm, D_FF // tf),
        in_specs=[pl.BlockSpec((tm, d), lambda i, j: (i, 0)), vec,
                  pl.BlockSpec((d, tf), lambda i, j: (0, j)),
                  pl.BlockSpec((tf, d), lambda i, j: (j, 0)), vec],
        out_specs=pl.BlockSpec((tm, d), lambda i, j: (i, 0)),
        out_shape=jax.ShapeDtypeStruct((n, d), F32),
        scratch_shapes=[pltpu.VMEM((tm, d), BF16), pltpu.VMEM((tm, d), F32)],
        compiler_params=_cparams(("parallel", "arbitrary")),
        name="mlp",
    )(x2, gpre, wu, wd, gpost)


def _encoder_layer(x, l, p):
    b, t, d = x.shape
    n = b * t
    x2 = x.reshape(n, d)
    n_attn = 3 * HEADS * HEAD_W
    w_in = p["w_in"][l]
    width = HEADS * HEAD_W
    attn_scale = jnp.concatenate([jnp.full((1, width), QK_DIM ** -0.5, F32),
                                  jnp.full((1, width), LOG2E, F32), jnp.ones((1, width), F32)], axis=1)
    qkv = _inproj(x2, p["g_mix_pre"][l:l + 1], w_in[:, :n_attn].astype(BF16), attn_scale, BF16)
    proj = _inproj(x2, p["g_mix_pre"][l:l + 1], w_in[:, n_attn:].astype(BF16),
                   jnp.ones((1, w_in.shape[1] - n_attn), F32), F32)
    lam_init = 0.8 - 0.6 * math.exp(-0.3 * l)
    o_a = _attention(qkv.reshape(b, t, n_attn), p["rel_bias"], p["g_attn_sub"][l:l + 1],
                     p["lam_q1"][l:l + 1], p["lam_k1"][l:l + 1], p["lam_q2"][l:l + 1],
                     p["lam_k2"][l:l + 1], lam_init)
    o_b = _hgrn2(proj.reshape(b, t, proj.shape[1]), p["lb_fwd"], p["lb_bwd"], p["g_hgrn_out"][l:l + 1])
    x1 = _merge(x2, o_a.reshape(n, d), o_b.reshape(n, d), proj,
                p["w_proj_a"][l].astype(BF16), p["w_proj_b"][l].astype(BF16),
                p["w_out"][l].astype(BF16), p["g_mix_post"][l:l + 1])
    y = _mlp(x1, p["g_mlp_pre"][l:l + 1], p["w_mlp_up"][l].astype(BF16),
             p["w_mlp_down"][l].astype(BF16), p["g_mlp_post"][l:l + 1])
    return y.reshape(b, t, d)


def kernel(x_prompt, x_sample, rel_bias, g_mix_pre, w_in, lam_q1, lam_k1, lam_q2, lam_k2, g_attn_sub, lb_fwd, lb_bwd, g_hgrn_out, w_proj_a, w_proj_b, w_out, g_mix_post, g_mlp_pre, w_mlp_up, w_mlp_down, g_mlp_post):
    p = dict(rel_bias=rel_bias, g_mix_pre=g_mix_pre, w_in=w_in, lam_q1=lam_q1, lam_k1=lam_k1,
             lam_q2=lam_q2, lam_k2=lam_k2, g_attn_sub=g_attn_sub, lb_fwd=lb_fwd, lb_bwd=lb_bwd,
             g_hgrn_out=g_hgrn_out, w_proj_a=w_proj_a, w_proj_b=w_proj_b, w_out=w_out,
             g_mix_post=g_mix_post, g_mlp_pre=g_mlp_pre, w_mlp_up=w_mlp_up, w_mlp_down=w_mlp_down,
             g_mlp_post=g_mlp_post)

    def trunk(x):
        for l in range(g_mix_pre.shape[0]):
            x = _encoder_layer(x, l, p)
        return x

    return (trunk(x_prompt), trunk(x_sample))
```
